```python
import jax, jax.numpy as jnp
from jax import lax
import numpy as np

D_MODEL = 2048
BATCH = 1
SEQ = 8192
DEPTH = 4

GRID_W = 64
CTX_LEN = 256
N_MIXERS = 3
MOD_CHUNKS = 6
NORM_EPS = 1e-6
CONV_W = 3
HEAD_DIM = 64
N_Q_HEADS = D_MODEL // HEAD_DIM
N_KV_HEADS = N_Q_HEADS // 8
GQA_GROUP = N_Q_HEADS // N_KV_HEADS
Q_WIDTH = N_Q_HEADS * HEAD_DIM
KV_WIDTH = N_KV_HEADS * HEAD_DIM
WINDOW = 128
ATTN_BLOCK = 128
ATTN_SCALE = HEAD_DIM ** -0.5
ROPE_BASE = 10000.0
ROPE_PAIRS = HEAD_DIM // 4
NEG_INF = -1e30
CHUNK = 128
N_SPATIAL_GROUPS = 8
N_EXPERTS = 32
TOP_K = 4
D_EXPERT = 768
SWIGLU_LIMIT = 7.0
SWIGLU_ALPHA = 1.702
MOE_BLOCK = 128

kernel_name = "hybrid_conv_swa_gmlp_moe_dit"


def _layer_plan(depth):
    counts = [0] * N_MIXERS
    plan = []
    for i in range(depth):
        kind = i % N_MIXERS
        plan.append((kind, counts[kind]))
        counts[kind] += 1
    return plan, counts


def _rms_norm(x, g):
    x32 = x.astype(jnp.float32)
    y = x32 * lax.rsqrt(jnp.mean(x32 * x32, axis=-1, keepdims=True) + NORM_EPS)
    return (y * g.astype(jnp.float32)).astype(x.dtype)


def _modulate(h, g, shift, scale):
    return _rms_norm(h, g) * (1 + scale) + shift


def _rope_tables(seq):
    rows = seq // GRID_W
    row_pos = jnp.broadcast_to(jnp.arange(rows, dtype=jnp.float32)[:, None], (rows, GRID_W)).reshape(-1)
    col_pos = jnp.broadcast_to(jnp.arange(GRID_W, dtype=jnp.float32)[None, :], (rows, GRID_W)).reshape(-1)
    inv_freq = ROPE_BASE ** (-jnp.arange(ROPE_PAIRS, dtype=jnp.float32) / ROPE_PAIRS)
    ang_r = row_pos[:, None] * inv_freq[None, :]
    ang_c = col_pos[:, None] * inv_freq[None, :]
    return (jnp.cos(ang_r), jnp.sin(ang_r), jnp.cos(ang_c), jnp.sin(ang_c))


def _rotate(x, cos, sin):
    x1, x2 = jnp.split(x, 2, axis=-1)
    cos = cos[None, :, None, :]
    sin = sin[None, :, None, :]
    return jnp.concatenate([x1 * cos - x2 * sin, x2 * cos + x1 * sin], axis=-1)


def _axial_rope(x, rope):
    cos_r, sin_r, cos_c, sin_c = rope
    x_row, x_col = jnp.split(x.astype(jnp.float32), 2, axis=-1)
    return jnp.concatenate([_rotate(x_row, cos_r, sin_r), _rotate(x_col, cos_c, sin_c)], axis=-1).astype(x.dtype)


def _short_conv(hn, w_in, conv_w, w_out):
    length = hn.shape[1]
    b_gate, c_gate, v = jnp.split(hn @ w_in, 3, axis=-1)
    u = c_gate * v
    half = CONV_W // 2
    up = jnp.pad(u, ((0, 0), (half, half), (0, 0)))
    y = sum(conv_w[j] * up[:, j:j + length] for j in range(CONV_W))
    return (b_gate * y) @ w_out


def _chunk_mlp(hn, w_in, b_in, g_v, w_s, b_s, w_out, b_out):
    bsz, length, d = hn.shape
    u, v = jnp.split(jax.nn.gelu(hn @ w_in + b_in), 2, axis=-1)
    v = _rms_norm(v, g_v).reshape(bsz, length // CHUNK, CHUNK, N_SPATIAL_GROUPS, d // N_SPATIAL_GROUPS)
    vs = jnp.einsum('gts,bnsgc->bntgc', w_s, v) + b_s.T[:, :, None]
    return (u * vs.reshape(bsz, length, d)) @ w_out + b_out


def _window_attention(hn, hn_c, w_qkv, b_qkv, g_q, g_k, sink, w_o, b_o, rope, need_ctx_out):
    bsz, seq, _ = hn.shape
    nb = seq // ATTN_BLOCK
    qkv = hn @ w_qkv + b_qkv
    q = qkv[..., :Q_WIDTH].reshape(bsz, seq, N_Q_HEADS, HEAD_DIM)
    k = qkv[..., Q_WIDTH:Q_WIDTH + KV_WIDTH].reshape(bsz, seq, N_KV_HEADS, HEAD_DIM)
    v = qkv[..., Q_WIDTH + KV_WIDTH:].reshape(bsz, seq, N_KV_HEADS, HEAD_DIM)
    q = _axial_rope(_rms_norm(q, g_q), rope)
    k = _axial_rope(_rms_norm(k, g_k), rope)
    kv_c = hn_c @ w_qkv[:, Q_WIDTH:] + b_qkv[Q_WIDTH:]
    k_c = _rms_norm(kv_c[..., :KV_WIDTH].reshape(bsz, -1, N_KV_HEADS, HEAD_DIM), g_k)
    v_c = kv_c[..., KV_WIDTH:].reshape(bsz, -1, N_KV_HEADS, HEAD_DIM)
    n_ctx = k_c.shape[1]
    sink_hg = sink.reshape(N_KV_HEADS, GQA_GROUP).astype(jnp.float32)

    def to_blocks(t):
        return jnp.moveaxis(t.reshape((bsz, -1, ATTN_BLOCK) + t.shape[2:]), 1, 0)

    def band(t):
        tp = to_blocks(jnp.pad(t, ((0, 0), (ATTN_BLOCK, ATTN_BLOCK), (0, 0), (0, 0))))
        return jnp.concatenate([tp[:-2], tp[1:-1], tp[2:]], axis=2)

    q_blocks = to_blocks(q).reshape(nb, bsz, ATTN_BLOCK, N_KV_HEADS, GQA_GROUP, HEAD_DIM)
    r = jnp.arange(ATTN_BLOCK)[:, None]
    kk = jnp.arange(3 * ATTN_BLOCK)[None, :]
    off = kk - r
    key_pos = jnp.arange(nb)[:, None, None] * ATTN_BLOCK + kk[None] - ATTN_BLOCK
    mask = (off >= ATTN_BLOCK - WINDOW) & (off <= ATTN_BLOCK + WINDOW) & (key_pos >= 0) & (key_pos < seq)

    def attend(args):
        qb, kb, vb, mb = args
        s_lat = jnp.einsum('bqhgd,bkhd->bhgqk', qb, kb, preferred_element_type=jnp.float32) * ATTN_SCALE
        s_lat = jnp.where(mb, s_lat, NEG_INF)
        s_ctx = jnp.einsum('bqhgd,bchd->bhgqc', qb, k_c, preferred_element_type=jnp.float32) * ATTN_SCALE
        s_sink = jnp.broadcast_to(sink_hg[None, :, :, None, None], s_lat.shape[:-1] + (1,))
        p = jax.nn.softmax(jnp.concatenate([s_lat, s_ctx, s_sink], axis=-1), axis=-1).astype(vb.dtype)
        nk = kb.shape[1]
        return (jnp.einsum('bhgqk,bkhd->bqhgd', p[..., :nk], vb)
                + jnp.einsum('bhgqc,bchd->bqhgd', p[..., nk:nk + n_ctx], v_c))

    o = lax.map(attend, (q_blocks, band(k), band(v), mask))
    out = jnp.moveaxis(o, 0, 1).reshape(bsz, seq, Q_WIDTH) @ w_o + b_o

    out_c = None
    if need_ctx_out:
        q_c = hn_c @ w_qkv[:, :Q_WIDTH] + b_qkv[:Q_WIDTH]
        q_c = _rms_norm(q_c.reshape(bsz, n_ctx, N_KV_HEADS, GQA_GROUP, HEAD_DIM), g_q)
        s = jnp.einsum('bqhgd,bkhd->bhgqk', q_c, k_c, preferred_element_type=jnp.float32) * ATTN_SCALE
        s_sink = jnp.broadcast_to(sink_hg[None, :, :, None, None], s.shape[:-1] + (1,))
        p = jax.nn.softmax(jnp.concatenate([s, s_sink], axis=-1), axis=-1).astype(v_c.dtype)
        o_c = jnp.einsum('bhgqk,bkhd->bqhgd', p[..., :n_ctx], v_c).reshape(bsz, n_ctx, Q_WIDTH)
        out_c = o_c @ w_o + b_o
    return out, out_c


def _expert_swiglu(xb, w1, b1, w2, b2):
    h = xb @ w1 + b1
    glu = jnp.minimum(h[:, ::2], SWIGLU_LIMIT)
    lin = jnp.clip(h[:, 1::2], -SWIGLU_LIMIT, SWIGLU_LIMIT)
    act = glu * jax.nn.sigmoid(SWIGLU_ALPHA * glu) * (lin + 1)
    return act @ w2 + b2


def _moe(t, r_w, r_b, w1, b1, w2, b2):
    n_tok, d = t.shape
    logits = (t @ r_w + r_b).astype(jnp.float32)
    top_val, top_idx = lax.top_k(logits, TOP_K)
    gate = jax.nn.softmax(top_val, axis=-1).astype(t.dtype)
    n_assign = n_tok * TOP_K
    flat_e = top_idx.reshape(n_assign)
    order = jnp.argsort(flat_e)
    e_sorted = flat_e[order]
    tok_sorted = order // TOP_K
    gate_sorted = gate.reshape(n_assign)[order]
    counts = jnp.bincount(flat_e, length=N_EXPERTS)
    padded = (counts + MOE_BLOCK - 1) // MOE_BLOCK * MOE_BLOCK
    grp_start = jnp.cumsum(counts) - counts
    pad_end = jnp.cumsum(padded)
    pad_start = pad_end - padded
    dest = pad_start[e_sorted] + jnp.arange(n_assign) - grp_start[e_sorted]
    n_blocks = -(-n_assign // MOE_BLOCK) + N_EXPERTS
    rows = jnp.zeros((n_blocks * MOE_BLOCK, d), t.dtype).at[dest].set(t[tok_sorted])
    block_e = jnp.minimum(jnp.searchsorted(pad_end, jnp.arange(n_blocks) * MOE_BLOCK, side='right'),
                          N_EXPERTS - 1)

    def expert_block(args):
        xb, e = args
        return _expert_swiglu(xb, w1[e], b1[e], w2[e], b2[e])

    y = lax.map(expert_block, (rows.reshape(n_blocks, MOE_BLOCK, d), block_e)).reshape(-1, d)
    contrib = y[dest] * gate_sorted[:, None]
    return jax.ops.segment_sum(contrib, tok_sorted, num_segments=n_tok)


def setup_inputs(seed: int = 0) -> dict:
    key = jax.random.key(seed)
    keys = iter(jax.random.split(key, 40))

    def normal(shape, scale):
        return jax.random.normal(next(keys), shape, jnp.float32) * scale

    def gain(shape):
        return 1.0 + normal(shape, 0.05)

    _, (n_a, n_b, n_c) = _layer_plan(DEPTH)
    d = D_MODEL
    qkv_w = Q_WIDTH + 2 * KV_WIDTH
    return {
        "x": normal((BATCH, SEQ, d), 1.0),
        "c": normal((BATCH, d), 1.0),
        "ctx": normal((BATCH, CTX_LEN, d), 1.0),
        "c_ctx": normal((d,), 1.0),
        "w_mod": normal((DEPTH, d, MOD_CHUNKS * d), 0.5 * d ** -0.5),
        "b_mod": normal((DEPTH, MOD_CHUNKS * d), 0.02),
        "g_norm1": gain((DEPTH, d)),
        "g_norm2": gain((DEPTH, d)),
        "a_w_in": normal((n_a, d, 3 * d), d ** -0.5),
        "a_conv": normal((n_a, CONV_W, d), CONV_W ** -0.5),
        "a_w_out": normal((n_a, d, d), d ** -0.5),
        "b_w_qkv": normal((n_b, d, qkv_w), d ** -0.5),
        "b_b_qkv": normal((n_b, qkv_w), 0.02),
        "b_g_q": gain((n_b, HEAD_DIM)),
        "b_g_k": gain((n_b, HEAD_DIM)),
        "b_sink": normal((n_b, N_Q_HEADS), 0.5),
        "b_w_o": normal((n_b, Q_WIDTH, d), Q_WIDTH ** -0.5),
        "b_b_o": normal((n_b, d), 0.02),
        "c_w_in": normal((n_c, d, 2 * d), d ** -0.5),
        "c_b_in": normal((n_c, 2 * d), 0.02),
        "c_g_v": gain((n_c, d)),
        "c_w_s": normal((n_c, N_SPATIAL_GROUPS, CHUNK, CHUNK), CHUNK ** -0.5),
        "c_b_s": 1.0 + normal((n_c, N_SPATIAL_GROUPS, CHUNK), 0.05),
        "c_w_out": normal((n_c, d, d), d ** -0.5),
        "c_b_out": normal((n_c, d), 0.02),
        "r_w": normal((DEPTH, d, N_EXPERTS), d ** -0.5),
        "r_b": normal((DEPTH, N_EXPERTS), 0.01),
        "e_w1": normal((DEPTH, N_EXPERTS, d, 2 * D_EXPERT), d ** -0.5),
        "e_b1": normal((DEPTH, N_EXPERTS, 2 * D_EXPERT), 0.02),
        "e_w2": normal((DEPTH, N_EXPERTS, D_EXPERT, d), D_EXPERT ** -0.5),
        "e_b2": normal((DEPTH, N_EXPERTS, d), 0.02),
    }


def reference(x, c, ctx, c_ctx, w_mod, b_mod, g_norm1, g_norm2, a_w_in, a_conv, a_w_out,
              b_w_qkv, b_b_qkv, b_g_q, b_g_k, b_sink, b_w_o, b_b_o,
              c_w_in, c_b_in, c_g_v, c_w_s, c_b_s, c_w_out, c_b_out,
              r_w, r_b, e_w1, e_b1, e_w2, e_b2):
    bsz, seq, d = x.shape
    plan, _ = _layer_plan(DEPTH)
    readers = [i for i, (kind, _) in enumerate(plan) if kind == 1]
    last_reader = readers[-1] if readers else -1
    rope = _rope_tables(seq)
    cond = jax.nn.silu(c)
    cond_c = jax.nn.silu(c_ctx)
    h, hc = x, ctx
    for i, (kind, s) in enumerate(plan):
        upd_ctx = i < last_reader
        read_ctx = i <= last_reader
        mod = jnp.split(cond @ w_mod[i] + b_mod[i], MOD_CHUNKS, axis=-1)
        sh1, sc1, gt1, sh2, sc2, gt2 = [m[:, None, :] for m in mod]
        hn = _modulate(h, g_norm1[i], sh1, sc1)
        hn_c = None
        if read_ctx:
            csh1, csc1, cgt1, csh2, csc2, cgt2 = jnp.split(cond_c @ w_mod[i] + b_mod[i], MOD_CHUNKS)
            hn_c = _modulate(hc, g_norm1[i], csh1, csc1)
        out_c = None
        if kind == 0:
            out = _short_conv(hn, a_w_in[s], a_conv[s], a_w_out[s])
            if upd_ctx:
                out_c = _short_conv(hn_c, a_w_in[s], a_conv[s], a_w_out[s])
        elif kind == 1:
            out, out_c = _window_attention(hn, hn_c, b_w_qkv[s], b_b_qkv[s], b_g_q[s], b_g_k[s], b_sink[s],
                                           b_w_o[s], b_b_o[s], rope, upd_ctx)
        else:
            out = _chunk_mlp(hn, c_w_in[s], c_b_in[s], c_g_v[s], c_w_s[s], c_b_s[s], c_w_out[s], c_b_out[s])
            if upd_ctx:
                out_c = _chunk_mlp(hn_c, c_w_in[s], c_b_in[s], c_g_v[s], c_w_s[s], c_b_s[s],
                                   c_w_out[s], c_b_out[s])
        h = h + gt1 * out
        hn2 = _modulate(h, g_norm2[i], sh2, sc2)
        if upd_ctx:
            hc = hc + cgt1 * out_c
            hn2_c = _modulate(hc, g_norm2[i], csh2, csc2)
            tokens = jnp.concatenate([hn2.reshape(-1, d), hn2_c.reshape(-1, d)], axis=0)
        else:
            tokens = hn2.reshape(-1, d)
        y = _moe(tokens, r_w[i], r_b[i], e_w1[i], e_b1[i], e_w2[i], e_b2[i])
        h = h + gt2 * y[:bsz * seq].reshape(bsz, seq, d)
        if upd_ctx:
            hc = hc + cgt2 * y[bsz * seq:].reshape(bsz, -1, d)
    return h
```

```python
import functools

import jax
import jax.numpy as jnp
from jax import lax
from jax.experimental import pallas as pl
from jax.experimental.pallas import tpu as pltpu

F32 = jnp.float32
BF16 = jnp.bfloat16
U32 = jnp.uint32
I32 = jnp.int32

NORM_EPS = 1e-6
GRID_W = 64
HEAD_DIM = 64
GQA_GROUP = 8
WINDOW = 128
ATTN_BLOCK = 128
ATTN_SCALE = HEAD_DIM ** -0.5
ROPE_BASE = 10000.0
ROPE_PAIRS = HEAD_DIM // 4
NEG_INF = -1e30
CHUNK = 128
N_SPATIAL_GROUPS = 8
TOP_K = 4
SWIGLU_LIMIT = 7.0
SWIGLU_ALPHA = 1.702
MOD_CHUNKS = 6
CONV_W = 3
N_MIXERS = 3

LANES = 128
MOE_ROWS = 256
VMEM_LIMIT = 56 * 1024 * 1024


def _params(semantics, vmem=VMEM_LIMIT):
    return pltpu.CompilerParams(dimension_semantics=semantics, vmem_limit_bytes=vmem)


def _bits(x):
    return pltpu.bitcast(x, U32)


def _pack_halves(lo, hi):
    lo_b = _bits(lo.astype(BF16).astype(F32))
    hi_b = _bits(hi.astype(BF16).astype(F32))
    return lax.shift_right_logical(lo_b, jnp.uint32(16)) | (hi_b & jnp.uint32(0xFFFF0000))


def _unpack_halves(w):
    lo = pltpu.bitcast(lax.shift_left(w, jnp.uint32(16)), F32)
    hi = pltpu.bitcast(w & jnp.uint32(0xFFFF0000), F32)
    return lo, hi


def _rms_scale(x):
    return lax.rsqrt(jnp.mean(x * x, axis=-1, keepdims=True) + NORM_EPS)


def _modulate(x, g, shift, scale):
    return (x * _rms_scale(x) * g) * (1.0 + scale) + shift


def _sigmoid(x):
    return 1.0 / (1.0 + jnp.exp(-x))


def _mod_kernel(cc_ref, w_ref, b_ref, o_ref):
    cc = cc_ref[...]
    cond = cc * _sigmoid(cc)
    o_ref[0] = jnp.dot(cond.astype(BF16), w_ref[0].astype(BF16),
                       preferred_element_type=F32) + b_ref[0]


def _mod_vectors(cc, w_mod, b_mod):
    depth, d, n = w_mod.shape
    tn = 1024
    return pl.pallas_call(
        _mod_kernel,
        out_shape=jax.ShapeDtypeStruct((depth, 8, n), F32),
        grid=(depth, n // tn),
        in_specs=[
            pl.BlockSpec((8, d), lambda l, j: (0, 0)),
            pl.BlockSpec((1, d, tn), lambda l, j: (l, 0, j)),
            pl.BlockSpec((1, 1, tn), lambda l, j: (l, 0, j)),
        ],
        out_specs=pl.BlockSpec((1, 8, tn), lambda l, j: (l, 0, j)),
        compiler_params=_params(("arbitrary", "arbitrary")),
        name="mod_vectors",
    )(cc, w_mod, b_mod.reshape(depth, 1, n))


def _prenorm_kernel(h_ref, g_ref, sh_ref, sc_ref, o_ref):
    o_ref[...] = _modulate(h_ref[...], g_ref[...], sh_ref[...], sc_ref[...]).astype(BF16)


def _prenorm(h, g, shift, scale):
    m, d = h.shape
    tm = 256
    vec = pl.BlockSpec((1, d), lambda i: (0, 0))
    return pl.pallas_call(
        _prenorm_kernel,
        out_shape=jax.ShapeDtypeStruct((m, d), BF16),
        grid=(m // tm,),
        in_specs=[pl.BlockSpec((tm, d), lambda i: (i, 0)), vec, vec, vec],
        out_specs=pl.BlockSpec((tm, d), lambda i: (i, 0)),
        compiler_params=_params(("arbitrary",)),
        name="prenorm",
    )(h, g, shift, scale)


def _mm_kernel(x_ref, w_ref, b_ref, o_ref, *, gelu):
    acc = jnp.dot(x_ref[...], w_ref[...], preferred_element_type=F32) + b_ref[...]
    if gelu:
        inner = 0.7978845608028654 * (acc + 0.044715 * (acc * acc * acc))
        acc = 0.5 * acc * (1.0 + jnp.tanh(inner))
    o_ref[...] = acc.astype(o_ref.dtype)


def _matmul(x, w, b, *, gelu=False):
    m, k = x.shape
    n = w.shape[1]
    tm = 1024 if m % 1024 == 0 else 256
    tn = next(t for t in (1024, 1280, 768, 512, 256) if n % t == 0)
    return pl.pallas_call(
        functools.partial(_mm_kernel, gelu=gelu),
        out_shape=jax.ShapeDtypeStruct((m, n), BF16),
        grid=(n // tn, m // tm),
        in_specs=[
            pl.BlockSpec((tm, k), lambda j, i: (i, 0)),
            pl.BlockSpec((k, tn), lambda j, i: (0, j)),
            pl.BlockSpec((1, tn), lambda j, i: (0, j)),
        ],
        out_specs=pl.BlockSpec((tm, tn), lambda j, i: (i, j)),
        compiler_params=_params(("arbitrary", "arbitrary")),
        name="matmul_gelu" if gelu else "matmul",
    )(x, w, b)


def _conv_kernel(b_ref, c_ref, v_ref, cp_ref, vp_ref, cn_ref, vn_ref, w_ref, o_ref, *, n_tiles):
    i = pl.program_id(0)
    tm = b_ref.shape[0]
    u = c_ref[...].astype(F32) * v_ref[...].astype(F32)
    u_prev = cp_ref[7:8, :].astype(F32) * vp_ref[7:8, :].astype(F32)
    u_next = cn_ref[0:1, :].astype(F32) * vn_ref[0:1, :].astype(F32)
    u_prev = jnp.where(i == 0, 0.0, u_prev)
    u_next = jnp.where(i == n_tiles - 1, 0.0, u_next)
    row = lax.broadcasted_iota(I32, u.shape, 0)
    below = jnp.where(row == 0, u_prev, pltpu.roll(u, 1, axis=0))
    above = jnp.where(row == tm - 1, u_next, pltpu.roll(u, tm - 1, axis=0))
    w = w_ref[...]
    y = w[0:1, :] * below + w[1:2, :] * u + w[2:3, :] * above
    o_ref[...] = (b_ref[...].astype(F32) * y).astype(BF16)


def _short_conv_gate(bcv, conv_w):
    m, n3 = bcv.shape
    d = n3 // 3
    tm, tc = 256, 512
    nt, nc = m // tm, d // tc
    hb = tm // 8
    last8 = m // 8 - 1
    return pl.pallas_call(
        functools.partial(_conv_kernel, n_tiles=nt),
        out_shape=jax.ShapeDtypeStruct((m, d), BF16),
        grid=(nt, nc),
        in_specs=[
            pl.BlockSpec((tm, tc), lambda i, j: (i, j)),
            pl.BlockSpec((tm, tc), lambda i, j: (i, j + nc)),
            pl.BlockSpec((tm, tc), lambda i, j: (i, j + 2 * nc)),
            pl.BlockSpec((8, tc), lambda i, j: (jnp.maximum(i * hb - 1, 0), j + nc)),
            pl.BlockSpec((8, tc), lambda i, j: (jnp.maximum(i * hb - 1, 0), j + 2 * nc)),
            pl.BlockSpec((8, tc), lambda i, j: (jnp.minimum((i + 1) * hb, last8), j + nc)),
            pl.BlockSpec((8, tc), lambda i, j: (jnp.minimum((i + 1) * hb, last8), j + 2 * nc)),
            pl.BlockSpec((CONV_W, tc), lambda i, j: (0, j)),
        ],
        out_specs=pl.BlockSpec((tm, tc), lambda i, j: (i, j)),
        compiler_params=_params(("arbitrary", "arbitrary")),
        name="short_conv_gate",
    )(bcv, bcv, bcv, bcv, bcv, bcv, bcv, conv_w)


def _head_sumsq(x, seg_ref):
    x2 = x * x
    hi = x2.astype(BF16)
    lo = (x2 - hi.astype(F32)).astype(BF16)
    seg = seg_ref[...]
    return (jnp.dot(hi, seg, preferred_element_type=F32)
            + jnp.dot(lo, seg, preferred_element_type=F32))


def _swap16(x):
    n = x.shape[-1]
    lane = lax.broadcasted_iota(I32, x.shape, x.ndim - 1)
    first = (lane % 32) < 16
    return jnp.where(first, pltpu.roll(x, n - 16, axis=x.ndim - 1), pltpu.roll(x, 16, axis=x.ndim - 1))


def _tile_lanes(x, width):
    return jnp.concatenate([x] * (width // x.shape[-1]), axis=-1)


def _kprep_kernel(k_ref, v_ref, g_ref, cos_ref, sin_ref, seg_ref, kd_ref, vlr_ref, *, rope):
    k = k_ref[...].astype(F32)
    width = k.shape[-1]
    ss = _head_sumsq(k, seg_ref)
    kn = k * lax.rsqrt(ss * (1.0 / HEAD_DIM) + NORM_EPS) * _tile_lanes(g_ref[...], width)
    if rope:
        kn = kn * _tile_lanes(cos_ref[...], width) + _swap16(kn) * _tile_lanes(sin_ref[...], width)
    v = v_ref[...].astype(F32)
    lane = lax.broadcasted_iota(I32, (k.shape[0], LANES), 1)
    left = lane < HEAD_DIM
    for p in range(width // LANES):
        kp = kn[:, p * LANES:(p + 1) * LANES]
        kr = pltpu.roll(kp, HEAD_DIM, axis=1)
        kd_ref[:, (2 * p) * LANES:(2 * p + 1) * LANES] = jnp.where(left, kp, kr).astype(BF16)
        kd_ref[:, (2 * p + 1) * LANES:(2 * p + 2) * LANES] = jnp.where(left, kr, kp).astype(BF16)
        vp = v[:, p * LANES:(p + 1) * LANES]
        vr = pltpu.roll(vp, HEAD_DIM, axis=1)
        base = 4 * p * LANES
        vlr_ref[:, base:base + LANES] = jnp.where(left, vp, 0.0).astype(BF16)
        vlr_ref[:, base + LANES:base + 2 * LANES] = jnp.where(left, 0.0, vr).astype(BF16)
        vlr_ref[:, base + 2 * LANES:base + 3 * LANES] = jnp.where(left, vr, 0.0).astype(BF16)
        vlr_ref[:, base + 3 * LANES:base + 4 * LANES] = jnp.where(left, 0.0, vp).astype(BF16)


def _key_value_prep(qkv, g_k, cos, sin, seg, *, q_width, kv_width, rope):
    m = qkv.shape[0]
    tm = 256
    n_kv = kv_width // HEAD_DIM
    kb = q_width // kv_width
    return pl.pallas_call(
        functools.partial(_kprep_kernel, rope=rope),
        out_shape=(jax.ShapeDtypeStruct((m, n_kv * LANES), BF16),
                   jax.ShapeDtypeStruct((m, n_kv * 2 * LANES), BF16)),
        grid=(m // tm,),
        in_specs=[
            pl.BlockSpec((tm, kv_width), lambda i: (i, kb)),
            pl.BlockSpec((tm, kv_width), lambda i: (i, kb + 1)),
            pl.BlockSpec((1, LANES), lambda i: (0, 0)),
            pl.BlockSpec((tm, LANES), lambda i: (i, 0)),
            pl.BlockSpec((tm, LANES), lambda i: (i, 0)),
            pl.BlockSpec((kv_width, kv_width), lambda i: (0, 0)),
        ],
        out_specs=(pl.BlockSpec((tm, n_kv * LANES), lambda i: (i, 0)),
                   pl.BlockSpec((tm, n_kv * 2 * LANES), lambda i: (i, 0))),
        compiler_params=_params(("arbitrary",)),
        name="key_value_prep_rope" if rope else "key_value_prep",
    )(qkv, qkv, g_k, cos, sin, seg)


def _attn_kernel(sink_ref, q_ref, cos_ref, sin_ref, g_ref, seg_ref,
                 kd0_ref, kd1_ref, kd2_ref, vl0_ref, vl1_ref, vl2_ref, kdc_ref, vlc_ref,
                 o_ref, *, seq, n_kv):
    n = pl.program_id(0)
    blk = ATTN_BLOCK
    gw = GQA_GROUP * HEAD_DIM
    r = lax.broadcasted_iota(I32, (blk, 3 * blk), 0)
    kk = lax.broadcasted_iota(I32, (blk, 3 * blk), 1)
    off = kk - r
    key_pos = n * blk + kk - blk
    mask = ((off >= blk - WINDOW) & (off <= blk + WINDOW) & (key_pos >= 0) & (key_pos < seq))
    lane = lax.broadcasted_iota(I32, (blk, LANES), 1)
    left = lane < HEAD_DIM
    cos = _tile_lanes(cos_ref[...], gw)
    sin = _tile_lanes(sin_ref[...], gw)
    gq = _tile_lanes(g_ref[...], gw)
    nt = (((1,), (1,)), ((), ()))
    for h in range(n_kv):
        q = q_ref[:, h * gw:(h + 1) * gw].astype(F32)
        ss = _head_sumsq(q, seg_ref)
        qn = q * lax.rsqrt(ss * (1.0 / HEAD_DIM) + NORM_EPS) * gq
        qn = ((qn * cos + _swap16(qn) * sin) * ATTN_SCALE).astype(BF16)
        kd = jnp.concatenate([ref[:, h * LANES:(h + 1) * LANES] for ref in (kd0_ref, kd1_ref, kd2_ref)], axis=0)
        kdc = kdc_ref[:, h * LANES:(h + 1) * LANES]
        vb = 2 * h * LANES
        vl = jnp.concatenate([ref[:, vb:vb + LANES] for ref in (vl0_ref, vl1_ref, vl2_ref)], axis=0)
        vr = jnp.concatenate([ref[:, vb + LANES:vb + 2 * LANES] for ref in (vl0_ref, vl1_ref, vl2_ref)], axis=0)
        vlc = vlc_ref[:, vb:vb + LANES]
        vrc = vlc_ref[:, vb + LANES:vb + 2 * LANES]
        for p in range(GQA_GROUP // 2):
            pair = qn[:, p * LANES:(p + 1) * LANES]
            acc = None
            for side in range(2):
                g = 2 * p + side
                qg = jnp.where(left, pair, 0.0) if side == 0 else jnp.where(left, 0.0, pair)
                qg = qg.astype(BF16)
                s = lax.dot_general(qg, kd, nt, preferred_element_type=F32)
                s = jnp.where(mask, s, NEG_INF)
                sc = lax.dot_general(qg, kdc, nt, preferred_element_type=F32)
                sk = sink_ref[h * GQA_GROUP + g]
                mx = jnp.maximum(jnp.maximum(jnp.max(s, axis=-1, keepdims=True),
                                             jnp.max(sc, axis=-1, keepdims=True)), sk)
                e = jnp.exp(s - mx)
                ec = jnp.exp(sc - mx)
                den = (jnp.sum(e, axis=-1, keepdims=True) + jnp.sum(ec, axis=-1, keepdims=True)
                       + jnp.exp(sk - mx))
                inv = 1.0 / den
                pb = (e * inv).astype(BF16)
                pc = (ec * inv).astype(BF16)
                part = (jnp.dot(pb, vl if side == 0 else vr, preferred_element_type=F32)
                        + jnp.dot(pc, vlc if side == 0 else vrc, preferred_element_type=F32))
                acc = part if acc is None else acc + part
            o_ref[:, h * gw + p * LANES:h * gw + (p + 1) * LANES] = acc.astype(BF16)


def _window_attention(qkv, kd, vlr, kdc, vlrc, sink, g_q, cos, sin, seg, *, q_width):
    seq = qkv.shape[0]
    n_kv = kd.shape[1] // LANES
    nb = seq // ATTN_BLOCK
    n_ctx = kdc.shape[0]
    gw = GQA_GROUP * HEAD_DIM

    def band(shift, width):
        return pl.BlockSpec((ATTN_BLOCK, width),
                            lambda n, s: (jnp.clip(n + shift, 0, nb - 1), 0))

    grid_spec = pltpu.PrefetchScalarGridSpec(
        num_scalar_prefetch=1,
        grid=(nb,),
        in_specs=[
            pl.BlockSpec((ATTN_BLOCK, q_width), lambda n, s: (n, 0)),
            pl.BlockSpec((ATTN_BLOCK, LANES), lambda n, s: (n, 0)),
            pl.BlockSpec((ATTN_BLOCK, LANES), lambda n, s: (n, 0)),
            pl.BlockSpec((1, LANES), lambda n, s: (0, 0)),
            pl.BlockSpec((gw, gw), lambda n, s: (0, 0)),
            band(-1, kd.shape[1]), band(0, kd.shape[1]), band(1, kd.shape[1]),
            band(-1, vlr.shape[1]), band(0, vlr.shape[1]), band(1, vlr.shape[1]),
            pl.BlockSpec((n_ctx, kdc.shape[1]), lambda n, s: (0, 0)),
            pl.BlockSpec((n_ctx, vlrc.shape[1]), lambda n, s: (0, 0)),
        ],
        out_specs=pl.BlockSpec((ATTN_BLOCK, q_width), lambda n, s: (n, 0)),
    )
    return pl.pallas_call(
        functools.partial(_attn_kernel, seq=seq, n_kv=n_kv),
        out_shape=jax.ShapeDtypeStruct((seq, q_width), BF16),
        grid_spec=grid_spec,
        compiler_params=_params(("arbitrary",)),
        name="window_attention",
    )(sink, qkv, cos, sin, g_q, seg, kd, kd, kd, vlr, vlr, vlr, kdc, vlrc)


def _sgu_kernel(u_ref, v_ref, g_ref, ws_ref, bs_ref, o_ref):
    v = v_ref[...].astype(F32)
    vn = (v * _rms_scale(v) * g_ref[...]).astype(BF16)
    d = v.shape[-1]
    gwid = d // N_SPATIAL_GROUPS
    for g in range(N_SPATIAL_GROUPS):
        cols = slice(g * gwid, (g + 1) * gwid)
        vs = jnp.dot(ws_ref[g].astype(BF16), vn[:, cols], preferred_element_type=F32) + bs_ref[:, g:g + 1]
        o_ref[:, cols] = (u_ref[:, cols].astype(F32) * vs).astype(BF16)


def _spatial_gate(uv, g_v, w_s, b_s_t):
    m, d2 = uv.shape
    d = d2 // 2
    return pl.pallas_call(
        _sgu_kernel,
        out_shape=jax.ShapeDtypeStruct((m, d), BF16),
        grid=(m // CHUNK,),
        in_specs=[
            pl.BlockSpec((CHUNK, d), lambda i: (i, 0)),
            pl.BlockSpec((CHUNK, d), lambda i: (i, 1)),
            pl.BlockSpec((1, d), lambda i: (0, 0)),
            pl.BlockSpec((N_SPATIAL_GROUPS, CHUNK, CHUNK), lambda i: (0, 0, 0)),
            pl.BlockSpec((CHUNK, N_SPATIAL_GROUPS), lambda i: (0, 0)),
        ],
        out_specs=pl.BlockSpec((CHUNK, d), lambda i: (i, 0)),
        compiler_params=_params(("arbitrary",)),
        name="spatial_gate",
    )(uv, uv, g_v, w_s, b_s_t)


def _out_router_kernel(z_ref, w_ref, b_ref, h_ref, gate_ref, g_ref, sh_ref, sc_ref,
                       rw_ref, rb_ref, cnt0_ref,
                       h1_ref, xw_ref, idx_ref, gts_ref, rank_ref, cnt_ref, run_ref):
    i = pl.program_id(0)
    tm, d = h_ref.shape
    n_exp = rw_ref.shape[0]

    @pl.when(i == 0)
    def _():
        run_ref[...] = cnt0_ref[...]

    out = jnp.dot(z_ref[...], w_ref[...], preferred_element_type=F32) + b_ref[...]
    h1 = h_ref[...] + gate_ref[...] * out
    h1_ref[...] = h1
    y = _modulate(h1, g_ref[...], sh_ref[...], sc_ref[...])
    yb = y.astype(BF16)
    xw_ref[...] = _pack_halves(y[:, :d // 2], y[:, d // 2:])

    logits = lax.dot_general(rw_ref[...], yb, (((1,), (1,)), ((), ())),
                             preferred_element_type=F32) + rb_ref[:, 0:1]
    e_iota = lax.broadcasted_iota(I32, (n_exp, tm), 0).astype(F32)
    vals, idxs = [], []
    cur = logits
    for _k in range(TOP_K):
        mx = jnp.max(cur, axis=0, keepdims=True)
        ix = jnp.min(jnp.where(cur == mx, e_iota, float(n_exp)), axis=0, keepdims=True)
        vals.append(mx)
        idxs.append(ix)
        cur = jnp.where(e_iota == ix, -jnp.inf, cur)
    exps = [jnp.exp(v - vals[0]) for v in vals]
    den = exps[0] + exps[1] + exps[2] + exps[3]
    sel = jnp.zeros((n_exp, tm), F32)
    for ix in idxs:
        sel = sel + jnp.where(e_iota == ix, 1.0, 0.0)
    before = (lax.broadcasted_iota(I32, (tm, tm), 0) < lax.broadcasted_iota(I32, (tm, tm), 1))
    prior = jnp.dot(sel.astype(BF16), jnp.where(before, 1.0, 0.0).astype(BF16),
                    preferred_element_type=F32) + run_ref[:, 0:1]
    zeros = jnp.zeros((8 - TOP_K, tm), F32)
    ranks = [jnp.sum(jnp.where(e_iota == ix, prior, 0.0), axis=0, keepdims=True) for ix in idxs]
    idx_ref[...] = jnp.concatenate(idxs + [zeros], axis=0).astype(I32)
    gts_ref[...] = jnp.concatenate([e / den for e in exps] + [zeros], axis=0)
    rank_ref[...] = jnp.concatenate(ranks + [zeros], axis=0).astype(I32)
    run_ref[...] = run_ref[...] + jnp.sum(sel, axis=1, keepdims=True)
    cnt_ref[...] = run_ref[...]


def _out_proj_router(z, w, b, h, gate, g2, sh2, sc2, r_w_t, r_b, cnt0):
    m, d = h.shape
    k = z.shape[1]
    n_exp = r_w_t.shape[0]
    tm = 256
    vec = pl.BlockSpec((1, d), lambda i: (0, 0))
    row = lambda width: pl.BlockSpec((tm, width), lambda i: (i, 0))
    tok = pl.BlockSpec((8, tm), lambda i: (0, i))
    cnt = pl.BlockSpec((n_exp, LANES), lambda i: (0, 0))
    return pl.pallas_call(
        _out_router_kernel,
        out_shape=(jax.ShapeDtypeStruct((m, d), F32),
                   jax.ShapeDtypeStruct((m, d // 2), U32),
                   jax.ShapeDtypeStruct((8, m), I32),
                   jax.ShapeDtypeStruct((8, m), F32),
                   jax.ShapeDtypeStruct((8, m), I32),
                   jax.ShapeDtypeStruct((n_exp, LANES), F32)),
        grid=(m // tm,),
        in_specs=[row(k), pl.BlockSpec((k, d), lambda i: (0, 0)), vec, row(d), vec, vec, vec, vec,
                  pl.BlockSpec((n_exp, d), lambda i: (0, 0)), cnt, cnt],
        out_specs=(row(d), row(d // 2), tok, tok, tok, cnt),
        scratch_shapes=[pltpu.VMEM((n_exp, LANES), F32)],
        compiler_params=_params(("arbitrary",)),
        name="out_proj_router",
    )(z, w, b, h, gate, g2, sh2, sc2, r_w_t, r_b, cnt0)


def _gather_rows(idx_ref, src_hbm, dst_ref, sem, n_rows):
    def body(r, carry):
        t = idx_ref[0, 0, r]
        pltpu.make_async_copy(src_hbm.at[pl.ds(t, 1)], dst_ref.at[pl.ds(r, 1)], sem).start()
        return carry
    lax.fori_loop(0, n_rows, body, 0)


def _wait_rows(src_hbm, dst_ref, sem, n_rows):
    pltpu.make_async_copy(src_hbm.at[pl.ds(0, n_rows)], dst_ref, sem).wait()


def _moe_up_kernel(be_ref, nu_ref, first_ref, next_ref, x_hbm, w1_ref, b1_ref, o_ref,
                   xbuf, sem, w1b):
    b = pl.program_id(0)
    n_used = nu_ref[0]
    rows = xbuf.shape[1]
    slot = b % 2

    @pl.when(b == 0)
    def _():
        _gather_rows(first_ref, x_hbm, xbuf.at[0], sem.at[0], rows)

    @pl.when(b + 1 < n_used)
    def _():
        _gather_rows(next_ref, x_hbm, xbuf.at[1 - slot], sem.at[1 - slot], rows)

    @pl.when(b >= n_used)
    def _():
        o_ref[...] = jnp.zeros(o_ref.shape, o_ref.dtype)

    @pl.when(b < n_used)
    def _():
        changed = jnp.logical_or(b == 0, be_ref[b] != be_ref[jnp.maximum(b - 1, 0)])

        @pl.when(changed)
        def _():
            w1b[...] = w1_ref[0].astype(BF16)

        _wait_rows(x_hbm, xbuf.at[slot], sem.at[slot], rows)
        lo, hi = _unpack_halves(xbuf[slot])
        half = lo.shape[1]
        hcat = (jnp.dot(lo.astype(BF16), w1b[0:half, :], preferred_element_type=F32)
                + jnp.dot(hi.astype(BF16), w1b[half:2 * half, :], preferred_element_type=F32)
                + b1_ref[0])
        glu = jnp.minimum(hcat, SWIGLU_LIMIT)
        gact = glu * _sigmoid(SWIGLU_ALPHA * glu)
        lin = jnp.clip(hcat, -SWIGLU_LIMIT, SWIGLU_LIMIT) + 1.0
        lane = lax.broadcasted_iota(I32, (rows, LANES), 1)
        even = (lane % 2) == 0
        for m in range(hcat.shape[1] // (2 * LANES)):
            a0, a1, a2 = 2 * m * LANES, (2 * m + 1) * LANES, (2 * m + 2) * LANES
            pa = gact[:, a0:a1] * pltpu.roll(lin[:, a0:a1], LANES - 1, axis=1)
            pb = pltpu.roll(gact[:, a1:a2], 1, axis=1) * lin[:, a1:a2]
            o_ref[:, m * LANES:(m + 1) * LANES] = jnp.where(even, pa, pb).astype(BF16)


def _moe_up(block_e, n_used, src_tok, xw, w1, b1):
    nb = src_tok.shape[0]
    rows = src_tok.shape[2]
    n_exp, d, f2 = w1.shape
    grid_spec = pltpu.PrefetchScalarGridSpec(
        num_scalar_prefetch=2,
        grid=(nb,),
        in_specs=[
            pl.BlockSpec((1, 1, rows), lambda b, be, nu: (0, 0, 0), memory_space=pltpu.SMEM),
            pl.BlockSpec((1, 1, rows), lambda b, be, nu: (jnp.minimum(b + 1, nb - 1), 0, 0),
                         memory_space=pltpu.SMEM),
            pl.BlockSpec(memory_space=pl.ANY),
            pl.BlockSpec((1, d, f2), lambda b, be, nu: (be[b], 0, 0)),
            pl.BlockSpec((1, 1, f2), lambda b, be, nu: (be[b], 0, 0)),
        ],
        out_specs=pl.BlockSpec((rows, f2 // 2), lambda b, be, nu: (b, 0)),
        scratch_shapes=[pltpu.VMEM((2, rows, d // 2), U32),
                        pltpu.SemaphoreType.DMA((2,)),
                        pltpu.VMEM((d, f2), BF16)],
    )
    return pl.pallas_call(
        _moe_up_kernel,
        out_shape=jax.ShapeDtypeStruct((nb * rows, f2 // 2), BF16),
        grid_spec=grid_spec,
        compiler_params=_params(("arbitrary",)),
        name="moe_up",
    )(block_e, n_used, src_tok, src_tok, xw, w1, b1.reshape(n_exp, 1, f2))


def _moe_down_kernel(be_ref, nu_ref, a_ref, w2_ref, b2_ref, o_ref, w2p):
    b = pl.program_id(0)
    n_used = nu_ref[0]

    @pl.when(b >= n_used)
    def _():
        o_ref[...] = jnp.zeros(o_ref.shape, o_ref.dtype)

    @pl.when(b < n_used)
    def _():
        changed = jnp.logical_or(b == 0, be_ref[b] != be_ref[jnp.maximum(b - 1, 0)])

        @pl.when(changed)
        def _():
            half = LANES // 2
            for m in range(w2p.shape[0] // LANES):
                lo = w2_ref[0, m * LANES:m * LANES + half, :]
                hi = w2_ref[0, m * LANES + half:(m + 1) * LANES, :]
                w2p[m * LANES:(m + 1) * LANES, :] = pltpu.bitcast(_pack_halves(lo, hi), BF16)

        y = jnp.dot(a_ref[...], w2p[...], preferred_element_type=F32) + b2_ref[0]
        d = y.shape[1]
        o_ref[...] = _pack_halves(y[:, :d // 2], y[:, d // 2:])


def _moe_down(block_e, n_used, act, w2, b2, rows):
    nb = act.shape[0] // rows
    n_exp, f, d = w2.shape
    clamp = lambda b, be, nu: (jnp.minimum(b, nu[0] - 1), 0)
    grid_spec = pltpu.PrefetchScalarGridSpec(
        num_scalar_prefetch=2,
        grid=(nb,),
        in_specs=[
            pl.BlockSpec((rows, f), clamp),
            pl.BlockSpec((1, f, d), lambda b, be, nu: (be[b], 0, 0)),
            pl.BlockSpec((1, 1, d), lambda b, be, nu: (be[b], 0, 0)),
        ],
        out_specs=pl.BlockSpec((rows, d // 2), lambda b, be, nu: (b, 0)),
        scratch_shapes=[pltpu.VMEM((f, d), BF16)],
    )
    return pl.pallas_call(
        _moe_down_kernel,
        out_shape=jax.ShapeDtypeStruct((nb * rows, d // 2), U32),
        grid_spec=grid_spec,
        compiler_params=_params(("arbitrary",)),
        name="moe_down",
    )(block_e, n_used, act, w2, b2.reshape(n_exp, 1, d))


def _combine_kernel(first_ref, next_ref, y_hbm, gts_ref, h_ref, gate_ref, g_ref, sh_ref, sc_ref,
                    h2_ref, *rest, n_tiles, with_norm):
    if with_norm:
        hn_ref, ybuf, sem = rest
    else:
        ybuf, sem = rest
    i = pl.program_id(0)
    tm, d = h_ref.shape
    rows = TOP_K * tm
    slot = i % 2

    @pl.when(i == 0)
    def _():
        _gather_rows(first_ref, y_hbm, ybuf.at[0], sem.at[0], rows)

    @pl.when(i + 1 < n_tiles)
    def _():
        _gather_rows(next_ref, y_hbm, ybuf.at[1 - slot], sem.at[1 - slot], rows)

    _wait_rows(y_hbm, ybuf.at[slot], sem.at[slot], rows)
    acc_lo = jnp.zeros((tm, d // 2), F32)
    acc_hi = jnp.zeros((tm, d // 2), F32)
    for k in range(TOP_K):
        lo, hi = _unpack_halves(ybuf[slot, k * tm:(k + 1) * tm, :])
        gk = gts_ref[:, k:k + 1]
        acc_lo = acc_lo + lo * gk
        acc_hi = acc_hi + hi * gk
    h2 = h_ref[...] + gate_ref[...] * jnp.concatenate([acc_lo, acc_hi], axis=1)
    h2_ref[...] = h2
    if with_norm:
        hn_ref[...] = _modulate(h2, g_ref[...], sh_ref[...], sc_ref[...]).astype(BF16)


def _moe_combine(dest, y, gts_t, h, gate, g1, sh1, sc1, *, with_norm):
    m, d = h.shape
    n_tiles = dest.shape[0]
    tm = m // n_tiles
    vec = pl.BlockSpec((1, d), lambda i: (0, 0))
    row = pl.BlockSpec((tm, d), lambda i: (i, 0))
    out_shape = [jax.ShapeDtypeStruct((m, d), F32)]
    out_specs = [row]
    if with_norm:
        out_shape.append(jax.ShapeDtypeStruct((m, d), BF16))
        out_specs.append(row)
    res = pl.pallas_call(
        functools.partial(_combine_kernel, n_tiles=n_tiles, with_norm=with_norm),
        out_shape=tuple(out_shape),
        grid=(n_tiles,),
        in_specs=[
            pl.BlockSpec((1, 1, TOP_K * tm), lambda i: (0, 0, 0), memory_space=pltpu.SMEM),
            pl.BlockSpec((1, 1, TOP_K * tm), lambda i: (jnp.minimum(i + 1, n_tiles - 1), 0, 0),
                         memory_space=pltpu.SMEM),
            pl.BlockSpec(memory_space=pl.ANY),
            pl.BlockSpec((tm, 8), lambda i: (i, 0)),
            row, vec, vec, vec, vec,
        ],
        out_specs=tuple(out_specs),
        scratch_shapes=[pltpu.VMEM((2, TOP_K * tm, d // 2), U32), pltpu.SemaphoreType.DMA((2,))],
        compiler_params=_params(("arbitrary",)),
        name="moe_combine_norm" if with_norm else "moe_combine",
    )(dest, dest, y, gts_t, h, gate, g1, sh1, sc1)
    return res if with_norm else (res[0], None)


def _dispatch_plan(idx, rank, counts, rows):
    n_exp = counts.shape[0]
    n_tok = idx.shape[1]
    n_assign = TOP_K * n_tok
    nb = n_assign // rows + n_exp
    blocks_e = (counts + rows - 1) // rows
    blk_end = jnp.cumsum(blocks_e)
    blk_start = blk_end - blocks_e
    n_used = blk_end[-1]
    dest = blk_start[idx] * rows + rank
    blk = jnp.minimum(jnp.arange(nb, dtype=I32), n_used - 1)
    block_e = jnp.minimum(jnp.searchsorted(blk_end, blk, side='right'), n_exp - 1).astype(I32)
    tok = jnp.broadcast_to(jnp.arange(n_tok, dtype=I32)[None, :], dest.shape)
    src_tok = jnp.zeros((nb * rows,), I32).at[dest.reshape(-1)].set(tok.reshape(-1))
    return block_e, n_used.reshape(1).astype(I32), src_tok.reshape(nb, 1, rows), dest


def _tile_dest(dest, tm):
    n_tok = dest.shape[1]
    return dest.reshape(TOP_K, n_tok // tm, tm).transpose(1, 0, 2).reshape(n_tok // tm, 1, TOP_K * tm)


def _rope_tables(seq):
    t = jnp.arange(seq, dtype=I32)
    row_pos = (t // GRID_W).astype(F32)
    col_pos = (t % GRID_W).astype(F32)
    inv_freq = ROPE_BASE ** (-jnp.arange(ROPE_PAIRS, dtype=F32) / ROPE_PAIRS)
    ang_r = row_pos[:, None] * inv_freq[None, :]
    ang_c = col_pos[:, None] * inv_freq[None, :]
    cos = jnp.concatenate([jnp.cos(ang_r), jnp.cos(ang_r), jnp.cos(ang_c), jnp.cos(ang_c)], axis=1)
    sin = jnp.concatenate([-jnp.sin(ang_r), jnp.sin(ang_r), -jnp.sin(ang_c), jnp.sin(ang_c)], axis=1)
    return jnp.concatenate([cos, cos], axis=1), jnp.concatenate([sin, sin], axis=1)


def _head_segments(width):
    seg = jnp.arange(width, dtype=I32) // HEAD_DIM
    return (seg[:, None] == seg[None, :]).astype(BF16)


def _layer_plan(depth):
    counts = [0] * N_MIXERS
    plan = []
    for i in range(depth):
        kind = i % N_MIXERS
        plan.append((kind, counts[kind]))
        counts[kind] += 1
    return plan


def kernel(x, c, ctx, c_ctx, w_mod, b_mod, g_norm1, g_norm2, a_w_in, a_conv, a_w_out, b_w_qkv, b_b_qkv, b_g_q, b_g_k, b_sink, b_w_o, b_b_o, c_w_in, c_b_in, c_g_v, c_w_s, c_b_s, c_w_out, c_b_out, r_w, r_b, e_w1, e_b1, e_w2, e_b2):
    bsz, seq, d = x.shape
    assert bsz == 1
    depth = w_mod.shape[0]
    n_exp = r_w.shape[2]
    n_ctx = ctx.shape[1]
    plan = _layer_plan(depth)
    readers = [i for i, (kind, _) in enumerate(plan) if kind == 1]
    last_reader = readers[-1] if readers else -1
    q_width = d
    kv_width = (b_w_qkv.shape[2] - q_width) // 2
    rows = MOE_ROWS
    tm_c = 128

    cc = jnp.concatenate([c, c_ctx[None, :], jnp.zeros((6, d), F32)], axis=0)
    mod = _mod_vectors(cc, w_mod, b_mod)

    def mod_vec(layer, stream, chunk):
        return mod[layer, stream:stream + 1, chunk * d:(chunk + 1) * d]

    def vec(a):
        return a.reshape(1, -1)

    cos, sin = _rope_tables(seq)
    seg_q = _head_segments(GQA_GROUP * HEAD_DIM)
    seg_k = _head_segments(kv_width)
    zero_bias = jnp.zeros((1, d), F32)
    zero_cnt = jnp.zeros((n_exp, LANES), F32)

    h = x[0]
    hc = ctx[0]
    hn = _prenorm(h, vec(g_norm1[0]), mod_vec(0, 0, 0), mod_vec(0, 0, 1))
    hn_c = None
    if 0 <= last_reader:
        hn_c = _prenorm(hc, vec(g_norm1[0]), mod_vec(0, 1, 0), mod_vec(0, 1, 1))

    for i, (kind, s) in enumerate(plan):
        upd_ctx = i < last_reader
        read_ctx = i <= last_reader
        streams = [(0, hn, h)] + ([(1, hn_c, hc)] if upd_ctx else [])

        zs = []
        if kind == 0:
            w_in = a_w_in[s].astype(BF16)
            w_out, b_out = a_w_out[s].astype(BF16), zero_bias
            for _, hn_s, _h in streams:
                bcv = _matmul(hn_s, w_in, jnp.zeros((1, 3 * d), F32))
                zs.append(_short_conv_gate(bcv, a_conv[s]))
        elif kind == 1:
            w_qkv = b_w_qkv[s].astype(BF16)
            w_out, b_out = b_w_o[s].astype(BF16), vec(b_b_o[s])
            g_q = vec(jnp.concatenate([b_g_q[s], b_g_q[s]]))
            g_k = vec(jnp.concatenate([b_g_k[s], b_g_k[s]]))
            qkv = _matmul(hn, w_qkv, vec(b_b_qkv[s]))
            qkv_c = _matmul(hn_c, w_qkv, vec(b_b_qkv[s]))
            kd, vlr = _key_value_prep(qkv, g_k, cos, sin, seg_k, q_width=q_width, kv_width=kv_width, rope=True)
            kdc, vlrc = _key_value_prep(qkv_c, g_k, cos[:n_ctx], sin[:n_ctx], seg_k,
                                        q_width=q_width, kv_width=kv_width, rope=False)
            zs.append(_window_attention(qkv, kd, vlr, kdc, vlrc, b_sink[s], g_q, cos, sin, seg_q,
                                        q_width=q_width))
            assert not upd_ctx
        else:
            w_in = c_w_in[s].astype(BF16)
            w_out, b_out = c_w_out[s].astype(BF16), vec(c_b_out[s])
            for _, hn_s, _h in streams:
                uv = _matmul(hn_s, w_in, vec(c_b_in[s]), gelu=True)
                zs.append(_spatial_gate(uv, vec(c_g_v[s]), c_w_s[s], c_b_s[s].T))

        r_w_t = r_w[i].T.astype(BF16)
        r_b_col = jnp.broadcast_to(r_b[i][:, None], (n_exp, LANES))
        outs = []
        cnt = zero_cnt
        for (st, _hn, h_s), z in zip(streams, zs):
            res = _out_proj_router(z, w_out, b_out, h_s, mod_vec(i, st, 2), vec(g_norm2[i]),
                                   mod_vec(i, st, 3), mod_vec(i, st, 4), r_w_t, r_b_col, cnt)
            outs.append(res)
            cnt = res[5]
        if len(outs) == 1:
            h1s, xw, idx, gts, rank = [outs[0][0]], outs[0][1], outs[0][2], outs[0][3], outs[0][4]
        else:
            h1s = [o[0] for o in outs]
            xw, idx, gts, rank = [jnp.concatenate([o[j] for o in outs], axis=(0 if j == 1 else 1))
                                  for j in (1, 2, 3, 4)]
        counts = cnt[:, 0].astype(I32)

        block_e, n_used, src_tok, dest = _dispatch_plan(idx[:TOP_K], rank[:TOP_K], counts, rows)
        act = _moe_up(block_e, n_used, src_tok, xw, e_w1[i], e_b1[i])
        y = _moe_down(block_e, n_used, act, e_w2[i], e_b2[i], rows)

        last = i == depth - 1
        new = []
        off = 0
        for (st, _hn, _h), h1 in zip(streams, h1s):
            n_tok = h1.shape[0]
            sl = slice(off, off + n_tok)
            off += n_tok
            nxt = i + 1
            need_norm = (not last) and (st == 0 or nxt <= last_reader)
            if need_norm:
                g1, sh1, sc1 = vec(g_norm1[nxt]), mod_vec(nxt, st, 0), mod_vec(nxt, st, 1)
            else:
                g1, sh1, sc1 = zero_bias, zero_bias, zero_bias
            h2, hn2 = _moe_combine(_tile_dest(dest[:, sl], tm_c), y, gts[:, sl].T,
                                   h1, mod_vec(i, st, 5), g1, sh1, sc1, with_norm=need_norm)
            new.append((h2, hn2))
        h, hn = new[0]
        if upd_ctx:
            hc, hn_c = new[1]
    return h[None]
```

```python
import functools

import jax
import jax.numpy as jnp
from jax import lax
from jax.experimental import pallas as pl
from jax.experimental.pallas import tpu as pltpu

F32 = jnp.float32
BF16 = jnp.bfloat16
U32 = jnp.uint32
I32 = jnp.int32

NORM_EPS = 1e-6
GRID_W = 64
HEAD_DIM = 64
GQA_GROUP = 8
WINDOW = 128
ATTN_BLOCK = 128
ATTN_SCALE = HEAD_DIM ** -0.5
ROPE_BASE = 10000.0
ROPE_PAIRS = HEAD_DIM // 4
NEG_INF = -1e30
CHUNK = 128
N_SPATIAL_GROUPS = 8
TOP_K = 4
SWIGLU_LIMIT = 7.0
SWIGLU_ALPHA = 1.702
MOD_CHUNKS = 6
CONV_W = 3
N_MIXERS = 3

LANES = 128
MOE_ROWS = 256
VMEM_LIMIT = 56 * 1024 * 1024


def _params(semantics, vmem=VMEM_LIMIT):
    return pltpu.CompilerParams(dimension_semantics=semantics, vmem_limit_bytes=vmem)


def _store_row_tiles(ref, val):
    rows, d = val.shape
    nch = d // LANES
    for j in range(nch):
        ref[pl.ds(j, rows, stride=nch), :] = val[:, j * LANES:(j + 1) * LANES]


def _load_row_tiles(ref, base, rows, nch, dtype):
    return jnp.concatenate(
        [ref[pl.ds(base * nch + j, rows, stride=nch), :].astype(dtype) for j in range(nch)], axis=1)


def _rms_scale(x):
    return lax.rsqrt(jnp.mean(x * x, axis=-1, keepdims=True) + NORM_EPS)


def _modulate(x, g, shift, scale):
    return (x * _rms_scale(x) * g) * (1.0 + scale) + shift


def _sigmoid(x):
    return 1.0 / (1.0 + jnp.exp(-x))


def _mod_kernel(cc_ref, w_ref, b_ref, o_ref):
    cc = cc_ref[...]
    cond = cc * _sigmoid(cc)
    o_ref[0] = jnp.dot(cond.astype(BF16), w_ref[0].astype(BF16),
                       preferred_element_type=F32) + b_ref[0]


def _mod_vectors(cc, w_mod, b_mod):
    depth, d, n = w_mod.shape
    tn = 1024
    return pl.pallas_call(
        _mod_kernel,
        out_shape=jax.ShapeDtypeStruct((depth, 8, n), F32),
        grid=(depth, n // tn),
        in_specs=[
            pl.BlockSpec((8, d), lambda l, j: (0, 0)),
            pl.BlockSpec((1, d, tn), lambda l, j: (l, 0, j)),
            pl.BlockSpec((1, 1, tn), lambda l, j: (l, 0, j)),
        ],
        out_specs=pl.BlockSpec((1, 8, tn), lambda l, j: (l, 0, j)),
        compiler_params=_params(("arbitrary", "arbitrary")),
        name="mod_vectors",
    )(cc, w_mod, b_mod.reshape(depth, 1, n))


def _prenorm_kernel(h_ref, g_ref, sh_ref, sc_ref, o_ref):
    o_ref[...] = _modulate(h_ref[...], g_ref[...], sh_ref[...], sc_ref[...]).astype(BF16)


def _prenorm(h, g, shift, scale):
    m, d = h.shape
    tm = 256
    vec = pl.BlockSpec((1, d), lambda i: (0, 0))
    return pl.pallas_call(
        _prenorm_kernel,
        out_shape=jax.ShapeDtypeStruct((m, d), BF16),
        grid=(m // tm,),
        in_specs=[pl.BlockSpec((tm, d), lambda i: (i, 0)), vec, vec, vec],
        out_specs=pl.BlockSpec((tm, d), lambda i: (i, 0)),
        compiler_params=_params(("arbitrary",)),
        name="prenorm",
    )(h, g, shift, scale)


def _mm_kernel(x_ref, w_ref, b_ref, o_ref, *, gelu):
    acc = jnp.dot(x_ref[...], w_ref[...], preferred_element_type=F32) + b_ref[...]
    if gelu:
        inner = 0.7978845608028654 * (acc + 0.044715 * (acc * acc * acc))
        acc = 0.5 * acc * (1.0 + jnp.tanh(inner))
    o_ref[...] = acc.astype(o_ref.dtype)


def _matmul(x, w, b, *, gelu=False):
    m, k = x.shape
    n = w.shape[1]
    tm = 1024 if m % 1024 == 0 else 256
    tn = next(t for t in (1024, 1280, 768, 512, 256) if n % t == 0)
    return pl.pallas_call(
        functools.partial(_mm_kernel, gelu=gelu),
        out_shape=jax.ShapeDtypeStruct((m, n), BF16),
        grid=(n // tn, m // tm),
        in_specs=[
            pl.BlockSpec((tm, k), lambda j, i: (i, 0)),
            pl.BlockSpec((k, tn), lambda j, i: (0, j)),
            pl.BlockSpec((1, tn), lambda j, i: (0, j)),
        ],
        out_specs=pl.BlockSpec((tm, tn), lambda j, i: (i, j)),
        compiler_params=_params(("arbitrary", "arbitrary")),
        name="matmul_gelu" if gelu else "matmul",
    )(x, w, b)


def _conv_kernel(b_ref, c_ref, v_ref, cp_ref, vp_ref, cn_ref, vn_ref, w_ref, o_ref, *, n_tiles):
    i = pl.program_id(0)
    tm = b_ref.shape[0]
    u = c_ref[...].astype(F32) * v_ref[...].astype(F32)
    u_prev = cp_ref[7:8, :].astype(F32) * vp_ref[7:8, :].astype(F32)
    u_next = cn_ref[0:1, :].astype(F32) * vn_ref[0:1, :].astype(F32)
    u_prev = jnp.where(i == 0, 0.0, u_prev)
    u_next = jnp.where(i == n_tiles - 1, 0.0, u_next)
    row = lax.broadcasted_iota(I32, u.shape, 0)
    below = jnp.where(row == 0, u_prev, pltpu.roll(u, 1, axis=0))
    above = jnp.where(row == tm - 1, u_next, pltpu.roll(u, tm - 1, axis=0))
    w = w_ref[...]
    y = w[0:1, :] * below + w[1:2, :] * u + w[2:3, :] * above
    o_ref[...] = (b_ref[...].astype(F32) * y).astype(BF16)


def _short_conv_gate(bcv, conv_w):
    m, n3 = bcv.shape
    d = n3 // 3
    tm = 512 if m % 512 == 0 else 256
    tc = 1024 if d % 1024 == 0 else 512
    nt, nc = m // tm, d // tc
    hb = tm // 8
    last8 = m // 8 - 1
    return pl.pallas_call(
        functools.partial(_conv_kernel, n_tiles=nt),
        out_shape=jax.ShapeDtypeStruct((m, d), BF16),
        grid=(nt, nc),
        in_specs=[
            pl.BlockSpec((tm, tc), lambda i, j: (i, j)),
            pl.BlockSpec((tm, tc), lambda i, j: (i, j + nc)),
            pl.BlockSpec((tm, tc), lambda i, j: (i, j + 2 * nc)),
            pl.BlockSpec((8, tc), lambda i, j: (jnp.maximum(i * hb - 1, 0), j + nc)),
            pl.BlockSpec((8, tc), lambda i, j: (jnp.maximum(i * hb - 1, 0), j + 2 * nc)),
            pl.BlockSpec((8, tc), lambda i, j: (jnp.minimum((i + 1) * hb, last8), j + nc)),
            pl.BlockSpec((8, tc), lambda i, j: (jnp.minimum((i + 1) * hb, last8), j + 2 * nc)),
            pl.BlockSpec((CONV_W, tc), lambda i, j: (0, j)),
        ],
        out_specs=pl.BlockSpec((tm, tc), lambda i, j: (i, j)),
        compiler_params=_params(("arbitrary", "arbitrary")),
        name="short_conv_gate",
    )(bcv, bcv, bcv, bcv, bcv, bcv, bcv, conv_w)


def _head_sumsq(x, seg_ref):
    x2 = x * x
    hi = x2.astype(BF16)
    lo = (x2 - hi.astype(F32)).astype(BF16)
    seg = seg_ref[...]
    return (jnp.dot(hi, seg, preferred_element_type=F32)
            + jnp.dot(lo, seg, preferred_element_type=F32))


def _swap16(x):
    n = x.shape[-1]
    lane = lax.broadcasted_iota(I32, x.shape, x.ndim - 1)
    first = (lane % 32) < 16
    return jnp.where(first, pltpu.roll(x, n - 16, axis=x.ndim - 1), pltpu.roll(x, 16, axis=x.ndim - 1))


def _tile_lanes(x, width):
    return jnp.concatenate([x] * (width // x.shape[-1]), axis=-1)


def _kprep_kernel(k_ref, v_ref, g_ref, cos_ref, sin_ref, seg_ref, kd_ref, vlr_ref, *, rope):
    k = k_ref[...].astype(F32)
    width = k.shape[-1]
    ss = _head_sumsq(k, seg_ref)
    kn = k * lax.rsqrt(ss * (1.0 / HEAD_DIM) + NORM_EPS) * _tile_lanes(g_ref[...], width)
    if rope:
        kn = kn * _tile_lanes(cos_ref[...], width) + _swap16(kn) * _tile_lanes(sin_ref[...], width)
    v = v_ref[...].astype(F32)
    lane = lax.broadcasted_iota(I32, (k.shape[0], LANES), 1)
    left = lane < HEAD_DIM
    for p in range(width // LANES):
        kp = kn[:, p * LANES:(p + 1) * LANES]
        kr = pltpu.roll(kp, HEAD_DIM, axis=1)
        kd_ref[:, (2 * p) * LANES:(2 * p + 1) * LANES] = jnp.where(left, kp, kr).astype(BF16)
        kd_ref[:, (2 * p + 1) * LANES:(2 * p + 2) * LANES] = jnp.where(left, kr, kp).astype(BF16)
        vp = v[:, p * LANES:(p + 1) * LANES]
        vr = pltpu.roll(vp, HEAD_DIM, axis=1)
        base = 4 * p * LANES
        vlr_ref[:, base:base + LANES] = jnp.where(left, vp, 0.0).astype(BF16)
        vlr_ref[:, base + LANES:base + 2 * LANES] = jnp.where(left, 0.0, vr).astype(BF16)
        vlr_ref[:, base + 2 * LANES:base + 3 * LANES] = jnp.where(left, vr, 0.0).astype(BF16)
        vlr_ref[:, base + 3 * LANES:base + 4 * LANES] = jnp.where(left, 0.0, vp).astype(BF16)


def _key_value_prep(qkv, g_k, cos, sin, seg, *, q_width, kv_width, rope):
    m = qkv.shape[0]
    tm = 256
    n_kv = kv_width // HEAD_DIM
    kb = q_width // kv_width
    return pl.pallas_call(
        functools.partial(_kprep_kernel, rope=rope),
        out_shape=(jax.ShapeDtypeStruct((m, n_kv * LANES), BF16),
                   jax.ShapeDtypeStruct((m, n_kv * 2 * LANES), BF16)),
        grid=(m // tm,),
        in_specs=[
            pl.BlockSpec((tm, kv_width), lambda i: (i, kb)),
            pl.BlockSpec((tm, kv_width), lambda i: (i, kb + 1)),
            pl.BlockSpec((1, LANES), lambda i: (0, 0)),
            pl.BlockSpec((tm, LANES), lambda i: (i, 0)),
            pl.BlockSpec((tm, LANES), lambda i: (i, 0)),
            pl.BlockSpec((kv_width, kv_width), lambda i: (0, 0)),
        ],
        out_specs=(pl.BlockSpec((tm, n_kv * LANES), lambda i: (i, 0)),
                   pl.BlockSpec((tm, n_kv * 2 * LANES), lambda i: (i, 0))),
        compiler_params=_params(("arbitrary",)),
        name="key_value_prep_rope" if rope else "key_value_prep",
    )(qkv, qkv, g_k, cos, sin, seg)


def _attn_kernel(sink_ref, q_ref, cos_ref, sin_ref, g_ref, seg_ref,
                 kd0_ref, kd1_ref, kd2_ref, vl0_ref, vl1_ref, vl2_ref, kdc_ref, vlc_ref,
                 o_ref, *, seq, n_kv):
    n = pl.program_id(0)
    blk = ATTN_BLOCK
    gw = GQA_GROUP * HEAD_DIM
    r = lax.broadcasted_iota(I32, (blk, 3 * blk), 0)
    kk = lax.broadcasted_iota(I32, (blk, 3 * blk), 1)
    off = kk - r
    key_pos = n * blk + kk - blk
    mask = ((off >= blk - WINDOW) & (off <= blk + WINDOW) & (key_pos >= 0) & (key_pos < seq))
    lane = lax.broadcasted_iota(I32, (blk, LANES), 1)
    left = lane < HEAD_DIM
    cos = _tile_lanes(cos_ref[...], gw)
    sin = _tile_lanes(sin_ref[...], gw)
    gq = _tile_lanes(g_ref[...], gw)
    nt = (((1,), (1,)), ((), ()))
    for h in range(n_kv):
        q = q_ref[:, h * gw:(h + 1) * gw].astype(F32)
        ss = _head_sumsq(q, seg_ref)
        qn = q * lax.rsqrt(ss * (1.0 / HEAD_DIM) + NORM_EPS) * gq
        qn = ((qn * cos + _swap16(qn) * sin) * ATTN_SCALE).astype(BF16)
        kd = jnp.concatenate([ref[:, h * LANES:(h + 1) * LANES] for ref in (kd0_ref, kd1_ref, kd2_ref)], axis=0)
        kdc = kdc_ref[:, h * LANES:(h + 1) * LANES]
        vb = 2 * h * LANES
        vl = jnp.concatenate([ref[:, vb:vb + LANES] for ref in (vl0_ref, vl1_ref, vl2_ref)], axis=0)
        vr = jnp.concatenate([ref[:, vb + LANES:vb + 2 * LANES] for ref in (vl0_ref, vl1_ref, vl2_ref)], axis=0)
        vlc = vlc_ref[:, vb:vb + LANES]
        vrc = vlc_ref[:, vb + LANES:vb + 2 * LANES]
        for p in range(GQA_GROUP // 2):
            pair = qn[:, p * LANES:(p + 1) * LANES]
            acc = None
            for side in range(2):
                g = 2 * p + side
                qg = jnp.where(left, pair, 0.0) if side == 0 else jnp.where(left, 0.0, pair)
                qg = qg.astype(BF16)
                s = lax.dot_general(qg, kd, nt, preferred_element_type=F32)
                s = jnp.where(mask, s, NEG_INF)
                sc = lax.dot_general(qg, kdc, nt, preferred_element_type=F32)
                sk = sink_ref[h * GQA_GROUP + g]
                mx = jnp.maximum(jnp.maximum(jnp.max(s, axis=-1, keepdims=True),
                                             jnp.max(sc, axis=-1, keepdims=True)), sk)
                e = jnp.exp(s - mx)
                ec = jnp.exp(sc - mx)
                den = (jnp.sum(e, axis=-1, keepdims=True) + jnp.sum(ec, axis=-1, keepdims=True)
                       + jnp.exp(sk - mx))
                inv = 1.0 / den
                pb = (e * inv).astype(BF16)
                pc = (ec * inv).astype(BF16)
                part = (jnp.dot(pb, vl if side == 0 else vr, preferred_element_type=F32)
                        + jnp.dot(pc, vlc if side == 0 else vrc, preferred_element_type=F32))
                acc = part if acc is None else acc + part
            o_ref[:, h * gw + p * LANES:h * gw + (p + 1) * LANES] = acc.astype(BF16)


def _window_attention(qkv, kd, vlr, kdc, vlrc, sink, g_q, cos, sin, seg, *, q_width):
    seq = qkv.shape[0]
    n_kv = kd.shape[1] // LANES
    nb = seq // ATTN_BLOCK
    n_ctx = kdc.shape[0]
    gw = GQA_GROUP * HEAD_DIM

    def band(shift, width):
        return pl.BlockSpec((ATTN_BLOCK, width),
                            lambda n, s: (jnp.clip(n + shift, 0, nb - 1), 0))

    grid_spec = pltpu.PrefetchScalarGridSpec(
        num_scalar_prefetch=1,
        grid=(nb,),
        in_specs=[
            pl.BlockSpec((ATTN_BLOCK, q_width), lambda n, s: (n, 0)),
            pl.BlockSpec((ATTN_BLOCK, LANES), lambda n, s: (n, 0)),
            pl.BlockSpec((ATTN_BLOCK, LANES), lambda n, s: (n, 0)),
            pl.BlockSpec((1, LANES), lambda n, s: (0, 0)),
            pl.BlockSpec((gw, gw), lambda n, s: (0, 0)),
            band(-1, kd.shape[1]), band(0, kd.shape[1]), band(1, kd.shape[1]),
            band(-1, vlr.shape[1]), band(0, vlr.shape[1]), band(1, vlr.shape[1]),
            pl.BlockSpec((n_ctx, kdc.shape[1]), lambda n, s: (0, 0)),
            pl.BlockSpec((n_ctx, vlrc.shape[1]), lambda n, s: (0, 0)),
        ],
        out_specs=pl.BlockSpec((ATTN_BLOCK, q_width), lambda n, s: (n, 0)),
    )
    return pl.pallas_call(
        functools.partial(_attn_kernel, seq=seq, n_kv=n_kv),
        out_shape=jax.ShapeDtypeStruct((seq, q_width), BF16),
        grid_spec=grid_spec,
        compiler_params=_params(("arbitrary",)),
        name="window_attention",
    )(sink, qkv, cos, sin, g_q, seg, kd, kd, kd, vlr, vlr, vlr, kdc, vlrc)


def _sgu_kernel(u_ref, v_ref, g_ref, ws_ref, bs_ref, o_ref):
    v = v_ref[...].astype(F32)
    vn = (v * _rms_scale(v) * g_ref[...]).astype(BF16)
    d = v.shape[-1]
    gwid = d // N_SPATIAL_GROUPS
    for g in range(N_SPATIAL_GROUPS):
        cols = slice(g * gwid, (g + 1) * gwid)
        vs = jnp.dot(ws_ref[g].astype(BF16), vn[:, cols], preferred_element_type=F32) + bs_ref[:, g:g + 1]
        o_ref[:, cols] = (u_ref[:, cols].astype(F32) * vs).astype(BF16)


def _spatial_gate(uv, g_v, w_s, b_s_t):
    m, d2 = uv.shape
    d = d2 // 2
    return pl.pallas_call(
        _sgu_kernel,
        out_shape=jax.ShapeDtypeStruct((m, d), BF16),
        grid=(m // CHUNK,),
        in_specs=[
            pl.BlockSpec((CHUNK, d), lambda i: (i, 0)),
            pl.BlockSpec((CHUNK, d), lambda i: (i, 1)),
            pl.BlockSpec((1, d), lambda i: (0, 0)),
            pl.BlockSpec((N_SPATIAL_GROUPS, CHUNK, CHUNK), lambda i: (0, 0, 0)),
            pl.BlockSpec((CHUNK, N_SPATIAL_GROUPS), lambda i: (0, 0)),
        ],
        out_specs=pl.BlockSpec((CHUNK, d), lambda i: (i, 0)),
        compiler_params=_params(("arbitrary",)),
        name="spatial_gate",
    )(uv, uv, g_v, w_s, b_s_t)


def _out_router_kernel(z_ref, w_ref, b_ref, h_ref, gate_ref, g_ref, sh_ref, sc_ref,
                       rw_ref, rb_ref, cnt0_ref,
                       h1_ref, xw_ref, idx_ref, gts_ref, rank_ref, cnt_ref, run_ref):
    i = pl.program_id(0)
    tm, d = h_ref.shape
    n_exp = rw_ref.shape[0]

    @pl.when(i == 0)
    def _():
        run_ref[...] = cnt0_ref[...]

    out = jnp.dot(z_ref[...], w_ref[...], preferred_element_type=F32) + b_ref[...]
    h1 = h_ref[...] + gate_ref[...] * out
    h1_ref[...] = h1
    y = _modulate(h1, g_ref[...], sh_ref[...], sc_ref[...])
    yb = y.astype(BF16)
    _store_row_tiles(xw_ref, y)

    logits = lax.dot_general(rw_ref[...], yb, (((1,), (1,)), ((), ())),
                             preferred_element_type=F32) + rb_ref[:, 0:1]
    e_iota = lax.broadcasted_iota(I32, (n_exp, tm), 0).astype(F32)
    vals, idxs = [], []
    cur = logits
    for _k in range(TOP_K):
        mx = jnp.max(cur, axis=0, keepdims=True)
        ix = jnp.min(jnp.where(cur == mx, e_iota, float(n_exp)), axis=0, keepdims=True)
        vals.append(mx)
        idxs.append(ix)
        cur = jnp.where(e_iota == ix, -jnp.inf, cur)
    exps = [jnp.exp(v - vals[0]) for v in vals]
    den = exps[0] + exps[1] + exps[2] + exps[3]
    sel = jnp.zeros((n_exp, tm), F32)
    for ix in idxs:
        sel = sel + jnp.where(e_iota == ix, 1.0, 0.0)
    before = (lax.broadcasted_iota(I32, (tm, tm), 0) < lax.broadcasted_iota(I32, (tm, tm), 1))
    prior = jnp.dot(sel.astype(BF16), jnp.where(before, 1.0, 0.0).astype(BF16),
                    preferred_element_type=F32) + run_ref[:, 0:1]
    zeros = jnp.zeros((8 - TOP_K, tm), F32)
    ranks = [jnp.sum(jnp.where(e_iota == ix, prior, 0.0), axis=0, keepdims=True) for ix in idxs]
    idx_ref[...] = jnp.concatenate(idxs + [zeros], axis=0).astype(I32)
    gts_ref[...] = jnp.concatenate([e / den for e in exps] + [zeros], axis=0)
    rank_ref[...] = jnp.concatenate(ranks + [zeros], axis=0).astype(I32)
    run_ref[...] = run_ref[...] + jnp.sum(sel, axis=1, keepdims=True)
    cnt_ref[...] = run_ref[...]


def _out_proj_router(z, w, b, h, gate, g2, sh2, sc2, r_w_t, r_b, cnt0):
    m, d = h.shape
    k = z.shape[1]
    n_exp = r_w_t.shape[0]
    tm = 256
    nch = d // LANES
    vec = pl.BlockSpec((1, d), lambda i: (0, 0))
    row = lambda width: pl.BlockSpec((tm, width), lambda i: (i, 0))
    tok = pl.BlockSpec((8, tm), lambda i: (0, i))
    cnt = pl.BlockSpec((n_exp, LANES), lambda i: (0, 0))
    return pl.pallas_call(
        _out_router_kernel,
        out_shape=(jax.ShapeDtypeStruct((m, d), F32),
                   jax.ShapeDtypeStruct((m * nch, LANES), F32),
                   jax.ShapeDtypeStruct((8, m), I32),
                   jax.ShapeDtypeStruct((8, m), F32),
                   jax.ShapeDtypeStruct((8, m), I32),
                   jax.ShapeDtypeStruct((n_exp, LANES), F32)),
        grid=(m // tm,),
        in_specs=[row(k), pl.BlockSpec((k, d), lambda i: (0, 0)), vec, row(d), vec, vec, vec, vec,
                  pl.BlockSpec((n_exp, d), lambda i: (0, 0)), cnt, cnt],
        out_specs=(row(d), pl.BlockSpec((tm * nch, LANES), lambda i: (i, 0)), tok, tok, tok, cnt),
        scratch_shapes=[pltpu.VMEM((n_exp, LANES), F32)],
        compiler_params=_params(("arbitrary",)),
        name="out_proj_router",
    )(z, w, b, h, gate, g2, sh2, sc2, r_w_t, r_b, cnt0)


GATHER_UNROLL = 16


def _gather_tokens(idx_ref, src_hbm, dst_ref, sem, n_tok, nch, *, unroll):
    def issue(r, prio):
        t = idx_ref[0, 0, r]
        src = src_hbm.at[pl.ds(pl.multiple_of(t * nch, nch), nch)]
        dst = dst_ref.at[pl.ds(pl.multiple_of(r * nch, nch), nch)]
        pltpu.make_async_copy(src, dst, sem).start(priority=prio)

    if unroll >= n_tok:
        for r in range(n_tok):
            issue(r, r % 2)
        return

    def body(c, carry):
        for u in range(unroll):
            issue(c * unroll + u, u % 2)
        return carry
    lax.fori_loop(0, n_tok // unroll, body, 0)


def _wait_tokens(src_hbm, dst_ref, sem):
    pltpu.make_async_copy(src_hbm.at[pl.ds(0, dst_ref.shape[0])], dst_ref, sem).wait()


def _moe_ffn_kernel(be_ref, nu_ref, nx_ref, first_ref, next_ref, x_hbm, w1_hbm, w2_hbm, b1_ref, b2_ref,
                    o_ref, xbuf, gsem, w1s, w2s, wsem, w1b, w2p, pair, *, layer, nch):
    b = pl.program_id(0)
    n_used = nu_ref[0]
    rows = MOE_ROWS
    slot = b % 2

    def weight_copies(e):
        return (pltpu.make_async_copy(w1_hbm.at[layer, e], w1s, wsem.at[0]),
                pltpu.make_async_copy(w2_hbm.at[layer, e], w2s, wsem.at[1]))

    @pl.when(b == 0)
    def _():
        for cp in weight_copies(be_ref[0]):
            cp.start()
        _gather_tokens(first_ref, x_hbm, xbuf.at[0], gsem.at[0], rows, nch, unroll=GATHER_UNROLL)

    @pl.when(b >= n_used)
    def _():
        o_ref[...] = jnp.zeros(o_ref.shape, o_ref.dtype)

    @pl.when(b < n_used)
    def _():
        e = be_ref[b]
        changed = jnp.logical_or(b == 0, e != be_ref[jnp.maximum(b - 1, 0)])

        @pl.when(changed)
        def _():
            for cp in weight_copies(e):
                cp.wait()
            w1b[...] = w1s[...].astype(BF16)
            half = LANES // 2
            for m in range(w2p.shape[0] // LANES):
                for c in range(nch):
                    cols = slice(c * LANES, (c + 1) * LANES)
                    pair[pl.ds(c * LANES, half, stride=2), :] = w2s[m * LANES:m * LANES + half, cols]
                    pair[pl.ds(c * LANES + 1, half, stride=2), :] = w2s[m * LANES + half:(m + 1) * LANES, cols]
                    w2p[m * LANES:(m + 1) * LANES, cols] = pair[c * LANES:(c + 1) * LANES, :].astype(BF16)
            nxt = nx_ref[b]

            @pl.when(nxt >= 0)
            def _():
                for cp in weight_copies(nxt):
                    cp.start()

        _gather_tokens(next_ref, x_hbm, xbuf.at[1 - slot], gsem.at[1 - slot], rows, nch, unroll=rows)
        _wait_tokens(x_hbm, xbuf.at[slot], gsem.at[slot])
        x = _load_row_tiles(xbuf.at[slot], 0, rows, nch, BF16)
        hcat = jnp.dot(x, w1b[...], preferred_element_type=F32) + b1_ref[0, 0]
        glu = jnp.minimum(hcat, SWIGLU_LIMIT)
        gact = glu * _sigmoid(SWIGLU_ALPHA * glu)
        lin = jnp.clip(hcat, -SWIGLU_LIMIT, SWIGLU_LIMIT) + 1.0
        lane = lax.broadcasted_iota(I32, (rows, LANES), 1)
        even = (lane % 2) == 0
        acts = []
        for m in range(hcat.shape[1] // (2 * LANES)):
            a0, a1, a2 = 2 * m * LANES, (2 * m + 1) * LANES, (2 * m + 2) * LANES
            pa = gact[:, a0:a1] * pltpu.roll(lin[:, a0:a1], LANES - 1, axis=1)
            pb = pltpu.roll(gact[:, a1:a2], 1, axis=1) * lin[:, a1:a2]
            acts.append(jnp.where(even, pa, pb).astype(BF16))
        act = jnp.concatenate(acts, axis=1)
        y = jnp.dot(act, w2p[...], preferred_element_type=F32) + b2_ref[0, 0]
        _store_row_tiles(o_ref, y)

        @pl.when(b == n_used - 1)
        def _():
            _wait_tokens(x_hbm, xbuf.at[1 - slot], gsem.at[1 - slot])


def _moe_ffn(block_e, n_used, next_e, src_tok, xrows, w1, b1, w2, b2, layer):
    nb = src_tok.shape[0]
    rows = src_tok.shape[2]
    depth, n_exp, d, f2 = w1.shape
    f = f2 // 2
    nch = d // LANES
    last_used = lambda b, be, nu, nx: jnp.minimum(b + 1, nu[0] - 1)
    grid_spec = pltpu.PrefetchScalarGridSpec(
        num_scalar_prefetch=3,
        grid=(nb,),
        in_specs=[
            pl.BlockSpec((1, 1, rows), lambda b, be, nu, nx: (0, 0, 0), memory_space=pltpu.SMEM),
            pl.BlockSpec((1, 1, rows), lambda b, be, nu, nx: (last_used(b, be, nu, nx), 0, 0),
                         memory_space=pltpu.SMEM),
            pl.BlockSpec(memory_space=pl.ANY),
            pl.BlockSpec(memory_space=pl.ANY),
            pl.BlockSpec(memory_space=pl.ANY),
            pl.BlockSpec((1, 1, 1, f2), lambda b, be, nu, nx: (layer, be[b], 0, 0)),
            pl.BlockSpec((1, 1, 1, d), lambda b, be, nu, nx: (layer, be[b], 0, 0)),
        ],
        out_specs=pl.BlockSpec((rows * nch, LANES), lambda b, be, nu, nx: (b, 0)),
        scratch_shapes=[pltpu.VMEM((2, rows * nch, LANES), F32),
                        pltpu.SemaphoreType.DMA((2,)),
                        pltpu.VMEM((d, f2), F32),
                        pltpu.VMEM((f, d), F32),
                        pltpu.SemaphoreType.DMA((2,)),
                        pltpu.VMEM((d, f2), BF16),
                        pltpu.VMEM((f, d), BF16),
                        pltpu.VMEM((nch * LANES, LANES), F32)],
    )
    return pl.pallas_call(
        functools.partial(_moe_ffn_kernel, layer=layer, nch=nch),
        out_shape=jax.ShapeDtypeStruct((nb * rows * nch, LANES), F32),
        grid_spec=grid_spec,
        compiler_params=_params(("arbitrary",)),
        name="moe_ffn",
    )(block_e, n_used, next_e, src_tok, src_tok, xrows, w1, w2,
      b1.reshape(depth, n_exp, 1, f2), b2.reshape(depth, n_exp, 1, d))


def _combine_kernel(first_ref, next_ref, y_hbm, gts_ref, h_ref, gate_ref, g_ref, sh_ref, sc_ref,
                    h2_ref, *rest, n_tiles, with_norm):
    if with_norm:
        hn_ref, ybuf, sem = rest
    else:
        ybuf, sem = rest
    i = pl.program_id(0)
    tm, d = h_ref.shape
    nch = d // LANES
    rows = TOP_K * tm
    slot = i % 2

    @pl.when(i == 0)
    def _():
        _gather_tokens(first_ref, y_hbm, ybuf.at[0], sem.at[0], rows, nch, unroll=GATHER_UNROLL)

    @pl.when(i + 1 < n_tiles)
    def _():
        _gather_tokens(next_ref, y_hbm, ybuf.at[1 - slot], sem.at[1 - slot], rows, nch, unroll=GATHER_UNROLL)

    _wait_tokens(y_hbm, ybuf.at[slot], sem.at[slot])
    gks = [gts_ref[:, k:k + 1] for k in range(TOP_K)]
    pieces = []
    for j in range(nch):
        acc = None
        for k in range(TOP_K):
            part = ybuf[slot, pl.ds(k * tm * nch + j, tm, stride=nch), :] * gks[k]
            acc = part if acc is None else acc + part
        pieces.append(acc)
    h2 = h_ref[...] + gate_ref[...] * jnp.concatenate(pieces, axis=1)
    h2_ref[...] = h2
    if with_norm:
        hn_ref[...] = _modulate(h2, g_ref[...], sh_ref[...], sc_ref[...]).astype(BF16)


def _moe_combine(dest, y, gts_t, h, gate, g1, sh1, sc1, *, with_norm):
    m, d = h.shape
    n_tiles = dest.shape[0]
    tm = m // n_tiles
    vec = pl.BlockSpec((1, d), lambda i: (0, 0))
    row = pl.BlockSpec((tm, d), lambda i: (i, 0))
    out_shape = [jax.ShapeDtypeStruct((m, d), F32)]
    out_specs = [row]
    if with_norm:
        out_shape.append(jax.ShapeDtypeStruct((m, d), BF16))
        out_specs.append(row)
    res = pl.pallas_call(
        functools.partial(_combine_kernel, n_tiles=n_tiles, with_norm=with_norm),
        out_shape=tuple(out_shape),
        grid=(n_tiles,),
        in_specs=[
            pl.BlockSpec((1, 1, TOP_K * tm), lambda i: (0, 0, 0), memory_space=pltpu.SMEM),
            pl.BlockSpec((1, 1, TOP_K * tm), lambda i: (jnp.minimum(i + 1, n_tiles - 1), 0, 0),
                         memory_space=pltpu.SMEM),
            pl.BlockSpec(memory_space=pl.ANY),
            pl.BlockSpec((tm, 8), lambda i: (i, 0)),
            row, vec, vec, vec, vec,
        ],
        out_specs=tuple(out_specs),
        scratch_shapes=[pltpu.VMEM((2, TOP_K * tm * (d // LANES), LANES), F32),
                        pltpu.SemaphoreType.DMA((2,))],
        compiler_params=_params(("arbitrary",)),
        name="moe_combine_norm" if with_norm else "moe_combine",
    )(dest, dest, y, gts_t, h, gate, g1, sh1, sc1)
    return res if with_norm else (res[0], None)


def _dispatch_plan(idx, rank, counts, rows):
    n_exp = counts.shape[0]
    n_tok = idx.shape[1]
    n_assign = TOP_K * n_tok
    nb = n_assign // rows + n_exp
    blocks_e = (counts + rows - 1) // rows
    blk_end = jnp.cumsum(blocks_e)
    blk_start = blk_end - blocks_e
    n_used = blk_end[-1]
    e_ids = jnp.arange(n_exp, dtype=I32)
    chosen = idx[None, :, :] == e_ids[:, None, None]
    dest = jnp.sum(jnp.where(chosen, blk_start[:, None, None], 0), axis=0) * rows + rank
    blk = jnp.minimum(jnp.arange(nb, dtype=I32), n_used - 1)
    block_e = jnp.minimum(jnp.sum((blk[:, None] >= blk_end[None, :]).astype(I32), axis=1), n_exp - 1)
    group_end = jnp.sum(jnp.where(block_e[:, None] == e_ids[None, :], blk_end[None, :], 0), axis=1)
    follower = jnp.sum(jnp.where(group_end[:, None] == jnp.arange(nb, dtype=I32)[None, :],
                                 block_e[None, :], 0), axis=1)
    next_e = jnp.where(group_end < n_used, follower, -1).astype(I32)
    tok = jnp.broadcast_to(jnp.arange(n_tok, dtype=I32)[None, :], dest.shape)
    src_tok = jnp.zeros((nb * rows,), I32).at[dest.reshape(-1)].set(tok.reshape(-1))
    return block_e.astype(I32), n_used.reshape(1).astype(I32), next_e, src_tok.reshape(nb, 1, rows), dest


def _tile_dest(dest, tm):
    n_tok = dest.shape[1]
    return dest.reshape(TOP_K, n_tok // tm, tm).transpose(1, 0, 2).reshape(n_tok // tm, 1, TOP_K * tm)


def _rope_tables(seq):
    t = jnp.arange(seq, dtype=I32)
    row_pos = (t // GRID_W).astype(F32)
    col_pos = (t % GRID_W).astype(F32)
    inv_freq = ROPE_BASE ** (-jnp.arange(ROPE_PAIRS, dtype=F32) / ROPE_PAIRS)
    ang_r = row_pos[:, None] * inv_freq[None, :]
    ang_c = col_pos[:, None] * inv_freq[None, :]
    cos = jnp.concatenate([jnp.cos(ang_r), jnp.cos(ang_r), jnp.cos(ang_c), jnp.cos(ang_c)], axis=1)
    sin = jnp.concatenate([-jnp.sin(ang_r), jnp.sin(ang_r), -jnp.sin(ang_c), jnp.sin(ang_c)], axis=1)
    return jnp.concatenate([cos, cos], axis=1), jnp.concatenate([sin, sin], axis=1)


def _head_segments(width):
    seg = jnp.arange(width, dtype=I32) // HEAD_DIM
    return (seg[:, None] == seg[None, :]).astype(BF16)


def _layer_plan(depth):
    counts = [0] * N_MIXERS
    plan = []
    for i in range(depth):
        kind = i % N_MIXERS
        plan.append((kind, counts[kind]))
        counts[kind] += 1
    return plan


def kernel(x, c, ctx, c_ctx, w_mod, b_mod, g_norm1, g_norm2, a_w_in, a_conv, a_w_out, b_w_qkv, b_b_qkv, b_g_q, b_g_k, b_sink, b_w_o, b_b_o, c_w_in, c_b_in, c_g_v, c_w_s, c_b_s, c_w_out, c_b_out, r_w, r_b, e_w1, e_b1, e_w2, e_b2):
    bsz, seq, d = x.shape
    assert bsz == 1
    depth = w_mod.shape[0]
    n_exp = r_w.shape[2]
    n_ctx = ctx.shape[1]
    plan = _layer_plan(depth)
    readers = [i for i, (kind, _) in enumerate(plan) if kind == 1]
    last_reader = readers[-1] if readers else -1
    q_width = d
    kv_width = (b_w_qkv.shape[2] - q_width) // 2
    rows = MOE_ROWS
    tm_c = 128

    cc = jnp.concatenate([c, c_ctx[None, :], jnp.zeros((6, d), F32)], axis=0)
    mod = _mod_vectors(cc, w_mod, b_mod)

    def mod_vec(layer, stream, chunk):
        return mod[layer, stream:stream + 1, chunk * d:(chunk + 1) * d]

    def vec(a):
        return a.reshape(1, -1)

    cos, sin = _rope_tables(seq)
    seg_q = _head_segments(GQA_GROUP * HEAD_DIM)
    seg_k = _head_segments(kv_width)
    zero_bias = jnp.zeros((1, d), F32)
    zero_cnt = jnp.zeros((n_exp, LANES), F32)

    h = x[0]
    hc = ctx[0]
    hn = _prenorm(h, vec(g_norm1[0]), mod_vec(0, 0, 0), mod_vec(0, 0, 1))
    hn_c = None
    if 0 <= last_reader:
        hn_c = _prenorm(hc, vec(g_norm1[0]), mod_vec(0, 1, 0), mod_vec(0, 1, 1))

    for i, (kind, s) in enumerate(plan):
        upd_ctx = i < last_reader
        read_ctx = i <= last_reader
        streams = [(0, hn, h)] + ([(1, hn_c, hc)] if upd_ctx else [])

        zs = []
        if kind == 0:
            w_in = a_w_in[s].astype(BF16)
            w_out, b_out = a_w_out[s].astype(BF16), zero_bias
            for _, hn_s, _h in streams:
                bcv = _matmul(hn_s, w_in, jnp.zeros((1, 3 * d), F32))
                zs.append(_short_conv_gate(bcv, a_conv[s]))
        elif kind == 1:
            w_qkv = b_w_qkv[s].astype(BF16)
            w_out, b_out = b_w_o[s].astype(BF16), vec(b_b_o[s])
            g_q = vec(jnp.concatenate([b_g_q[s], b_g_q[s]]))
            g_k = vec(jnp.concatenate([b_g_k[s], b_g_k[s]]))
            qkv = _matmul(hn, w_qkv, vec(b_b_qkv[s]))
            qkv_c = _matmul(hn_c, w_qkv, vec(b_b_qkv[s]))
            kd, vlr = _key_value_prep(qkv, g_k, cos, sin, seg_k, q_width=q_width, kv_width=kv_width, rope=True)
            kdc, vlrc = _key_value_prep(qkv_c, g_k, cos[:n_ctx], sin[:n_ctx], seg_k,
                                        q_width=q_width, kv_width=kv_width, rope=False)
            zs.append(_window_attention(qkv, kd, vlr, kdc, vlrc, b_sink[s], g_q, cos, sin, seg_q,
                                        q_width=q_width))
            assert not upd_ctx
        else:
            w_in = c_w_in[s].astype(BF16)
            w_out, b_out = c_w_out[s].astype(BF16), vec(c_b_out[s])
            for _, hn_s, _h in streams:
                uv = _matmul(hn_s, w_in, vec(c_b_in[s]), gelu=True)
                zs.append(_spatial_gate(uv, vec(c_g_v[s]), c_w_s[s], c_b_s[s].T))

        r_w_t = r_w[i].T.astype(BF16)
        r_b_col = jnp.broadcast_to(r_b[i][:, None], (n_exp, LANES))
        outs = []
        cnt = zero_cnt
        for (st, _hn, h_s), z in zip(streams, zs):
            res = _out_proj_router(z, w_out, b_out, h_s, mod_vec(i, st, 2), vec(g_norm2[i]),
                                   mod_vec(i, st, 3), mod_vec(i, st, 4), r_w_t, r_b_col, cnt)
            outs.append(res)
            cnt = res[5]
        if len(outs) == 1:
            h1s, xw, idx, gts, rank = [outs[0][0]], outs[0][1], outs[0][2], outs[0][3], outs[0][4]
        else:
            h1s = [o[0] for o in outs]
            xw, idx, gts, rank = [jnp.concatenate([o[j] for o in outs], axis=(0 if j == 1 else 1))
                                  for j in (1, 2, 3, 4)]
        counts = cnt[:, 0].astype(I32)

        block_e, n_used, next_e, src_tok, dest = _dispatch_plan(idx[:TOP_K], rank[:TOP_K], counts, rows)
        y = _moe_ffn(block_e, n_used, next_e, src_tok, xw, e_w1, e_b1, e_w2, e_b2, i)

        last = i == depth - 1
        new = []
        off = 0
        for (st, _hn, _h), h1 in zip(streams, h1s):
            n_tok = h1.shape[0]
            sl = slice(off, off + n_tok)
            off += n_tok
            nxt = i + 1
            need_norm = (not last) and (st == 0 or nxt <= last_reader)
            if need_norm:
                g1, sh1, sc1 = vec(g_norm1[nxt]), mod_vec(nxt, st, 0), mod_vec(nxt, st, 1)
            else:
                g1, sh1, sc1 = zero_bias, zero_bias, zero_bias
            h2, hn2 = _moe_combine(_tile_dest(dest[:, sl], tm_c), y, gts[:, sl].T,
                                   h1, mod_vec(i, st, 5), g1, sh1, sc1, with_norm=need_norm)
            new.append((h2, hn2))
        h, hn = new[0]
        if upd_ctx:
            hc, hn_c = new[1]
    return h[None]
```

```python
import functools

import jax
import jax.numpy as jnp
from jax import lax
from jax.experimental import pallas as pl
from jax.experimental.pallas import tpu as pltpu

F32 = jnp.float32
BF16 = jnp.bfloat16
U32 = jnp.uint32
I32 = jnp.int32

NORM_EPS = 1e-6
GRID_W = 64
HEAD_DIM = 64
GQA_GROUP = 8
WINDOW = 128
ATTN_BLOCK = 128
ATTN_SCALE = HEAD_DIM ** -0.5
ROPE_BASE = 10000.0
ROPE_PAIRS = HEAD_DIM // 4
NEG_INF = -1e30
LOG2_E = 1.4426950408889634
CHUNK = 128
N_SPATIAL_GROUPS = 8
TOP_K = 4
SWIGLU_LIMIT = 7.0
SWIGLU_ALPHA = 1.702
MOD_CHUNKS = 6
CONV_W = 3
N_MIXERS = 3

LANES = 128
MOE_ROWS = 256
VMEM_LIMIT = 56 * 1024 * 1024


def _params(semantics, vmem=VMEM_LIMIT):
    return pltpu.CompilerParams(dimension_semantics=semantics, vmem_limit_bytes=vmem)


def _store_row_tiles(ref, val):
    rows, d = val.shape
    nch = d // LANES
    for j in range(nch):
        ref[pl.ds(j, rows, stride=nch), :] = val[:, j * LANES:(j + 1) * LANES]


def _load_row_tiles(ref, base, rows, nch, dtype):
    return jnp.concatenate(
        [ref[pl.ds(base * nch + j, rows, stride=nch), :].astype(dtype) for j in range(nch)], axis=1)


def _rms_scale(x):
    return lax.rsqrt(jnp.mean(x * x, axis=-1, keepdims=True) + NORM_EPS)


def _modulate(x, g, shift, scale):
    return (x * _rms_scale(x) * g) * (1.0 + scale) + shift


def _sigmoid(x):
    return 1.0 / (1.0 + jnp.exp(-x))


def _mod_kernel(cc_ref, w_ref, b_ref, o_ref):
    cc = cc_ref[...]
    cond = cc * _sigmoid(cc)
    o_ref[0] = jnp.dot(cond.astype(BF16), w_ref[0].astype(BF16),
                       preferred_element_type=F32) + b_ref[0]


def _mod_vectors(cc, w_mod, b_mod):
    depth, d, n = w_mod.shape
    tn = 1024
    return pl.pallas_call(
        _mod_kernel,
        out_shape=jax.ShapeDtypeStruct((depth, 8, n), F32),
        grid=(depth, n // tn),
        in_specs=[
            pl.BlockSpec((8, d), lambda l, j: (0, 0)),
            pl.BlockSpec((1, d, tn), lambda l, j: (l, 0, j)),
            pl.BlockSpec((1, 1, tn), lambda l, j: (l, 0, j)),
        ],
        out_specs=pl.BlockSpec((1, 8, tn), lambda l, j: (l, 0, j)),
        compiler_params=_params(("arbitrary", "arbitrary")),
        name="mod_vectors",
    )(cc, w_mod, b_mod.reshape(depth, 1, n))


def _prenorm_kernel(h_ref, g_ref, sh_ref, sc_ref, o_ref):
    o_ref[...] = _modulate(h_ref[...], g_ref[...], sh_ref[...], sc_ref[...]).astype(BF16)


def _prenorm(h, g, shift, scale):
    m, d = h.shape
    tm = 256
    vec = pl.BlockSpec((1, d), lambda i: (0, 0))
    return pl.pallas_call(
        _prenorm_kernel,
        out_shape=jax.ShapeDtypeStruct((m, d), BF16),
        grid=(m // tm,),
        in_specs=[pl.BlockSpec((tm, d), lambda i: (i, 0)), vec, vec, vec],
        out_specs=pl.BlockSpec((tm, d), lambda i: (i, 0)),
        compiler_params=_params(("arbitrary",)),
        name="prenorm",
    )(h, g, shift, scale)


def _mm_kernel(x_ref, w_ref, b_ref, o_ref, *, gelu):
    acc = jnp.dot(x_ref[...], w_ref[...], preferred_element_type=F32) + b_ref[...]
    if gelu:
        inner = 0.7978845608028654 * (acc + 0.044715 * (acc * acc * acc))
        acc = 0.5 * acc * (1.0 + jnp.tanh(inner))
    o_ref[...] = acc.astype(o_ref.dtype)


def _matmul(x, w, b, *, gelu=False):
    m, k = x.shape
    n = w.shape[1]
    tm = 1024 if m % 1024 == 0 else 256
    tn = next(t for t in (1024, 1280, 768, 512, 256) if n % t == 0)
    return pl.pallas_call(
        functools.partial(_mm_kernel, gelu=gelu),
        out_shape=jax.ShapeDtypeStruct((m, n), BF16),
        grid=(n // tn, m // tm),
        in_specs=[
            pl.BlockSpec((tm, k), lambda j, i: (i, 0)),
            pl.BlockSpec((k, tn), lambda j, i: (0, j)),
            pl.BlockSpec((1, tn), lambda j, i: (0, j)),
        ],
        out_specs=pl.BlockSpec((tm, tn), lambda j, i: (i, j)),
        compiler_params=_params(("arbitrary", "arbitrary")),
        name="matmul_gelu" if gelu else "matmul",
    )(x, w, b)


def _conv_kernel(b_ref, c_ref, v_ref, cp_ref, vp_ref, cn_ref, vn_ref, w_ref, o_ref, *, n_tiles):
    i = pl.program_id(0)
    tm = b_ref.shape[0]
    u = c_ref[...].astype(F32) * v_ref[...].astype(F32)
    u_prev = cp_ref[7:8, :].astype(F32) * vp_ref[7:8, :].astype(F32)
    u_next = cn_ref[0:1, :].astype(F32) * vn_ref[0:1, :].astype(F32)
    u_prev = jnp.where(i == 0, 0.0, u_prev)
    u_next = jnp.where(i == n_tiles - 1, 0.0, u_next)
    row = lax.broadcasted_iota(I32, u.shape, 0)
    below = jnp.where(row == 0, u_prev, pltpu.roll(u, 1, axis=0))
    above = jnp.where(row == tm - 1, u_next, pltpu.roll(u, tm - 1, axis=0))
    w = w_ref[...]
    y = w[0:1, :] * below + w[1:2, :] * u + w[2:3, :] * above
    o_ref[...] = (b_ref[...].astype(F32) * y).astype(BF16)


def _short_conv_gate(bcv, conv_w):
    m, n3 = bcv.shape
    d = n3 // 3
    tm = 512 if m % 512 == 0 else 256
    tc = 1024 if d % 1024 == 0 else 512
    nt, nc = m // tm, d // tc
    hb = tm // 8
    last8 = m // 8 - 1
    return pl.pallas_call(
        functools.partial(_conv_kernel, n_tiles=nt),
        out_shape=jax.ShapeDtypeStruct((m, d), BF16),
        grid=(nt, nc),
        in_specs=[
            pl.BlockSpec((tm, tc), lambda i, j: (i, j)),
            pl.BlockSpec((tm, tc), lambda i, j: (i, j + nc)),
            pl.BlockSpec((tm, tc), lambda i, j: (i, j + 2 * nc)),
            pl.BlockSpec((8, tc), lambda i, j: (jnp.maximum(i * hb - 1, 0), j + nc)),
            pl.BlockSpec((8, tc), lambda i, j: (jnp.maximum(i * hb - 1, 0), j + 2 * nc)),
            pl.BlockSpec((8, tc), lambda i, j: (jnp.minimum((i + 1) * hb, last8), j + nc)),
            pl.BlockSpec((8, tc), lambda i, j: (jnp.minimum((i + 1) * hb, last8), j + 2 * nc)),
            pl.BlockSpec((CONV_W, tc), lambda i, j: (0, j)),
        ],
        out_specs=pl.BlockSpec((tm, tc), lambda i, j: (i, j)),
        compiler_params=_params(("arbitrary", "arbitrary")),
        name="short_conv_gate",
    )(bcv, bcv, bcv, bcv, bcv, bcv, bcv, conv_w)


def _head_sumsq(x, seg_ref):
    x2 = x * x
    hi = x2.astype(BF16)
    lo = (x2 - hi.astype(F32)).astype(BF16)
    seg = seg_ref[...]
    return (jnp.dot(hi, seg, preferred_element_type=F32)
            + jnp.dot(lo, seg, preferred_element_type=F32))


def _swap16(x):
    n = x.shape[-1]
    lane = lax.broadcasted_iota(I32, x.shape, x.ndim - 1)
    first = (lane % 32) < 16
    return jnp.where(first, pltpu.roll(x, n - 16, axis=x.ndim - 1), pltpu.roll(x, 16, axis=x.ndim - 1))


def _tile_lanes(x, width):
    return jnp.concatenate([x] * (width // x.shape[-1]), axis=-1)


def _kprep_kernel(k_ref, v_ref, g_ref, cos_ref, sin_ref, seg_ref, kd_ref, vlr_ref, *, rope):
    k = k_ref[...].astype(F32)
    width = k.shape[-1]
    ss = _head_sumsq(k, seg_ref)
    kn = k * lax.rsqrt(ss * (1.0 / HEAD_DIM) + NORM_EPS) * _tile_lanes(g_ref[...], width)
    if rope:
        kn = kn * _tile_lanes(cos_ref[...], width) + _swap16(kn) * _tile_lanes(sin_ref[...], width)
    v = v_ref[...].astype(F32)
    lane = lax.broadcasted_iota(I32, (k.shape[0], LANES), 1)
    left = lane < HEAD_DIM
    for p in range(width // LANES):
        kp = kn[:, p * LANES:(p + 1) * LANES]
        kr = pltpu.roll(kp, HEAD_DIM, axis=1)
        kd_ref[:, (2 * p) * LANES:(2 * p + 1) * LANES] = jnp.where(left, kp, kr).astype(BF16)
        kd_ref[:, (2 * p + 1) * LANES:(2 * p + 2) * LANES] = jnp.where(left, kr, kp).astype(BF16)
        vp = v[:, p * LANES:(p + 1) * LANES]
        vr = pltpu.roll(vp, HEAD_DIM, axis=1)
        base = 4 * p * LANES
        vlr_ref[:, base:base + LANES] = jnp.where(left, vp, 0.0).astype(BF16)
        vlr_ref[:, base + LANES:base + 2 * LANES] = jnp.where(left, 0.0, vr).astype(BF16)
        vlr_ref[:, base + 2 * LANES:base + 3 * LANES] = jnp.where(left, vr, 0.0).astype(BF16)
        vlr_ref[:, base + 3 * LANES:base + 4 * LANES] = jnp.where(left, 0.0, vp).astype(BF16)


def _key_value_prep(qkv, g_k, cos, sin, seg, *, q_width, kv_width, rope):
    m = qkv.shape[0]
    tm = 256
    n_kv = kv_width // HEAD_DIM
    kb = q_width // kv_width
    return pl.pallas_call(
        functools.partial(_kprep_kernel, rope=rope),
        out_shape=(jax.ShapeDtypeStruct((m, n_kv * LANES), BF16),
                   jax.ShapeDtypeStruct((m, n_kv * 2 * LANES), BF16)),
        grid=(m // tm,),
        in_specs=[
            pl.BlockSpec((tm, kv_width), lambda i: (i, kb)),
            pl.BlockSpec((tm, kv_width), lambda i: (i, kb + 1)),
            pl.BlockSpec((1, LANES), lambda i: (0, 0)),
            pl.BlockSpec((tm, LANES), lambda i: (i, 0)),
            pl.BlockSpec((tm, LANES), lambda i: (i, 0)),
            pl.BlockSpec((kv_width, kv_width), lambda i: (0, 0)),
        ],
        out_specs=(pl.BlockSpec((tm, n_kv * LANES), lambda i: (i, 0)),
                   pl.BlockSpec((tm, n_kv * 2 * LANES), lambda i: (i, 0))),
        compiler_params=_params(("arbitrary",)),
        name="key_value_prep_rope" if rope else "key_value_prep",
    )(qkv, qkv, g_k, cos, sin, seg)


def _attn_kernel(sink_ref, q_ref, cos_ref, sin_ref, g_ref, seg_ref,
                 kd0_ref, kd1_ref, kd2_ref, vl0_ref, vl1_ref, vl2_ref, kdc_ref, vlc_ref,
                 o_ref, s_scr, p_scr, *, seq, n_kv):
    n = pl.program_id(0)
    blk = ATTN_BLOCK
    n_band = 3 * blk
    n_keys = s_scr.shape[1]
    gw = GQA_GROUP * HEAD_DIM
    r = lax.broadcasted_iota(I32, (blk, 3 * blk), 0)
    kk = lax.broadcasted_iota(I32, (blk, 3 * blk), 1)
    off = kk - r
    key_pos = n * blk + kk - blk
    mask = ((off >= blk - WINDOW) & (off <= blk + WINDOW) & (key_pos >= 0) & (key_pos < seq))
    lane = lax.broadcasted_iota(I32, (blk, LANES), 1)
    left = lane < HEAD_DIM
    cos = _tile_lanes(cos_ref[...], gw)
    sin = _tile_lanes(sin_ref[...], gw)
    gq = _tile_lanes(g_ref[...], gw)
    nt = (((1,), (1,)), ((), ()))
    for h in range(n_kv):
        q = q_ref[:, h * gw:(h + 1) * gw].astype(F32)
        ss = _head_sumsq(q, seg_ref)
        qn = q * lax.rsqrt(ss * (1.0 / HEAD_DIM) + NORM_EPS) * gq
        qn = ((qn * cos + _swap16(qn) * sin) * (ATTN_SCALE * LOG2_E)).astype(BF16)
        kd = jnp.concatenate([ref[:, h * LANES:(h + 1) * LANES] for ref in (kd0_ref, kd1_ref, kd2_ref)], axis=0)
        kdc = kdc_ref[:, h * LANES:(h + 1) * LANES]
        vb = 2 * h * LANES
        vl = jnp.concatenate([ref[:, vb:vb + LANES] for ref in (vl0_ref, vl1_ref, vl2_ref)], axis=0)
        vr = jnp.concatenate([ref[:, vb + LANES:vb + 2 * LANES] for ref in (vl0_ref, vl1_ref, vl2_ref)], axis=0)
        vlc = vlc_ref[:, vb:vb + LANES]
        vrc = vlc_ref[:, vb + LANES:vb + 2 * LANES]

        for g in range(GQA_GROUP):
            pair = qn[:, (g // 2) * LANES:(g // 2 + 1) * LANES]
            qg = jnp.where(left, pair, 0.0) if g % 2 == 0 else jnp.where(left, 0.0, pair)
            s = lax.dot_general(qg, kd, nt, preferred_element_type=F32)
            s_scr[g * blk:(g + 1) * blk, 0:n_band] = jnp.where(mask, s, NEG_INF)
            s_scr[g * blk:(g + 1) * blk, n_band:n_keys] = lax.dot_general(qg, kdc, nt, preferred_element_type=F32)

        sk = jnp.concatenate([jnp.full((blk, 1), sink_ref[h * GQA_GROUP + g] * LOG2_E, F32)
                              for g in range(GQA_GROUP)], axis=0)
        mx = jnp.maximum(jnp.max(s_scr[...], axis=-1, keepdims=True), sk)
        e = jnp.exp2(s_scr[...] - mx)
        s_scr[...] = e
        inv = 1.0 / (jnp.sum(e, axis=-1, keepdims=True) + jnp.exp2(sk - mx))
        p_scr[...] = (s_scr[...] * inv).astype(BF16)

        for p in range(GQA_GROUP // 2):
            r0, r1, r2 = 2 * p * blk, (2 * p + 1) * blk, (2 * p + 2) * blk
            acc = (jnp.dot(p_scr[r0:r1, 0:n_band], vl, preferred_element_type=F32)
                   + jnp.dot(p_scr[r0:r1, n_band:n_keys], vlc, preferred_element_type=F32)
                   + jnp.dot(p_scr[r1:r2, 0:n_band], vr, preferred_element_type=F32)
                   + jnp.dot(p_scr[r1:r2, n_band:n_keys], vrc, preferred_element_type=F32))
            o_ref[:, h * gw + p * LANES:h * gw + (p + 1) * LANES] = acc.astype(BF16)


def _window_attention(qkv, kd, vlr, kdc, vlrc, sink, g_q, cos, sin, seg, *, q_width):
    seq = qkv.shape[0]
    n_kv = kd.shape[1] // LANES
    nb = seq // ATTN_BLOCK
    n_ctx = kdc.shape[0]
    gw = GQA_GROUP * HEAD_DIM

    def band(shift, width):
        return pl.BlockSpec((ATTN_BLOCK, width),
                            lambda n, s: (jnp.clip(n + shift, 0, nb - 1), 0))

    grid_spec = pltpu.PrefetchScalarGridSpec(
        num_scalar_prefetch=1,
        grid=(nb,),
        in_specs=[
            pl.BlockSpec((ATTN_BLOCK, q_width), lambda n, s: (n, 0)),
            pl.BlockSpec((ATTN_BLOCK, LANES), lambda n, s: (n, 0)),
            pl.BlockSpec((ATTN_BLOCK, LANES), lambda n, s: (n, 0)),
            pl.BlockSpec((1, LANES), lambda n, s: (0, 0)),
            pl.BlockSpec((gw, gw), lambda n, s: (0, 0)),
            band(-1, kd.shape[1]), band(0, kd.shape[1]), band(1, kd.shape[1]),
            band(-1, vlr.shape[1]), band(0, vlr.shape[1]), band(1, vlr.shape[1]),
            pl.BlockSpec((n_ctx, kdc.shape[1]), lambda n, s: (0, 0)),
            pl.BlockSpec((n_ctx, vlrc.shape[1]), lambda n, s: (0, 0)),
        ],
        out_specs=pl.BlockSpec((ATTN_BLOCK, q_width), lambda n, s: (n, 0)),
        scratch_shapes=[pltpu.VMEM((GQA_GROUP * ATTN_BLOCK, 3 * ATTN_BLOCK + n_ctx), F32),
                        pltpu.VMEM((GQA_GROUP * ATTN_BLOCK, 3 * ATTN_BLOCK + n_ctx), BF16)],
    )
    return pl.pallas_call(
        functools.partial(_attn_kernel, seq=seq, n_kv=n_kv),
        out_shape=jax.ShapeDtypeStruct((seq, q_width), BF16),
        grid_spec=grid_spec,
        compiler_params=_params(("arbitrary",)),
        name="window_attention",
    )(sink, qkv, cos, sin, g_q, seg, kd, kd, kd, vlr, vlr, vlr, kdc, vlrc)


def _sgu_kernel(u_ref, v_ref, g_ref, ws_ref, bs_ref, o_ref):
    v = v_ref[...].astype(F32)
    vn = (v * _rms_scale(v) * g_ref[...]).astype(BF16)
    d = v.shape[-1]
    gwid = d // N_SPATIAL_GROUPS
    for g in range(N_SPATIAL_GROUPS):
        cols = slice(g * gwid, (g + 1) * gwid)
        vs = jnp.dot(ws_ref[g].astype(BF16), vn[:, cols], preferred_element_type=F32) + bs_ref[:, g:g + 1]
        o_ref[:, cols] = (u_ref[:, cols].astype(F32) * vs).astype(BF16)


def _spatial_gate(uv, g_v, w_s, b_s_t):
    m, d2 = uv.shape
    d = d2 // 2
    return pl.pallas_call(
        _sgu_kernel,
        out_shape=jax.ShapeDtypeStruct((m, d), BF16),
        grid=(m // CHUNK,),
        in_specs=[
            pl.BlockSpec((CHUNK, d), lambda i: (i, 0)),
            pl.BlockSpec((CHUNK, d), lambda i: (i, 1)),
            pl.BlockSpec((1, d), lambda i: (0, 0)),
            pl.BlockSpec((N_SPATIAL_GROUPS, CHUNK, CHUNK), lambda i: (0, 0, 0)),
            pl.BlockSpec((CHUNK, N_SPATIAL_GROUPS), lambda i: (0, 0)),
        ],
        out_specs=pl.BlockSpec((CHUNK, d), lambda i: (i, 0)),
        compiler_params=_params(("arbitrary",)),
        name="spatial_gate",
    )(uv, uv, g_v, w_s, b_s_t)


def _out_router_kernel(z_ref, w_ref, b_ref, h_ref, gate_ref, g_ref, sh_ref, sc_ref,
                       rw_ref, rb_ref, cnt0_ref,
                       h1_ref, xw_ref, idx_ref, gts_ref, rank_ref, cnt_ref, run_ref):
    i = pl.program_id(0)
    tm, d = h_ref.shape
    n_exp = rw_ref.shape[0]

    @pl.when(i == 0)
    def _():
        run_ref[...] = cnt0_ref[...]

    out = jnp.dot(z_ref[...], w_ref[...], preferred_element_type=F32) + b_ref[...]
    h1 = h_ref[...] + gate_ref[...] * out
    h1_ref[...] = h1
    y = _modulate(h1, g_ref[...], sh_ref[...], sc_ref[...])
    yb = y.astype(BF16)
    _store_row_tiles(xw_ref, y)

    logits = lax.dot_general(rw_ref[...], yb, (((1,), (1,)), ((), ())),
                             preferred_element_type=F32) + rb_ref[:, 0:1]
    e_iota = lax.broadcasted_iota(I32, (n_exp, tm), 0).astype(F32)
    vals, idxs = [], []
    cur = logits
    for _k in range(TOP_K):
        mx = jnp.max(cur, axis=0, keepdims=True)
        ix = jnp.min(jnp.where(cur == mx, e_iota, float(n_exp)), axis=0, keepdims=True)
        vals.append(mx)
        idxs.append(ix)
        cur = jnp.where(e_iota == ix, -jnp.inf, cur)
    exps = [jnp.exp(v - vals[0]) for v in vals]
    den = exps[0] + exps[1] + exps[2] + exps[3]
    sel = jnp.zeros((n_exp, tm), F32)
    for ix in idxs:
        sel = sel + jnp.where(e_iota == ix, 1.0, 0.0)
    before = (lax.broadcasted_iota(I32, (tm, tm), 0) < lax.broadcasted_iota(I32, (tm, tm), 1))
    prior = jnp.dot(sel.astype(BF16), jnp.where(before, 1.0, 0.0).astype(BF16),
                    preferred_element_type=F32) + run_ref[:, 0:1]
    zeros = jnp.zeros((8 - TOP_K, tm), F32)
    ranks = [jnp.sum(jnp.where(e_iota == ix, prior, 0.0), axis=0, keepdims=True) for ix in idxs]
    idx_ref[...] = jnp.concatenate(idxs + [zeros], axis=0).astype(I32)
    gts_ref[...] = jnp.concatenate([e / den for e in exps] + [zeros], axis=0)
    rank_ref[...] = jnp.concatenate(ranks + [zeros], axis=0).astype(I32)
    run_ref[...] = run_ref[...] + jnp.sum(sel, axis=1, keepdims=True)
    cnt_ref[...] = run_ref[...]


def _out_proj_router(z, w, b, h, gate, g2, sh2, sc2, r_w_t, r_b, cnt0):
    m, d = h.shape
    k = z.shape[1]
    n_exp = r_w_t.shape[0]
    tm = 256
    nch = d // LANES
    vec = pl.BlockSpec((1, d), lambda i: (0, 0))
    row = lambda width: pl.BlockSpec((tm, width), lambda i: (i, 0))
    tok = pl.BlockSpec((8, tm), lambda i: (0, i))
    cnt = pl.BlockSpec((n_exp, LANES), lambda i: (0, 0))
    return pl.pallas_call(
        _out_router_kernel,
        out_shape=(jax.ShapeDtypeStruct((m, d), F32),
                   jax.ShapeDtypeStruct((m * nch, LANES), F32),
                   jax.ShapeDtypeStruct((8, m), I32),
                   jax.ShapeDtypeStruct((8, m), F32),
                   jax.ShapeDtypeStruct((8, m), I32),
                   jax.ShapeDtypeStruct((n_exp, LANES), F32)),
        grid=(m // tm,),
        in_specs=[row(k), pl.BlockSpec((k, d), lambda i: (0, 0)), vec, row(d), vec, vec, vec, vec,
                  pl.BlockSpec((n_exp, d), lambda i: (0, 0)), cnt, cnt],
        out_specs=(row(d), pl.BlockSpec((tm * nch, LANES), lambda i: (i, 0)), tok, tok, tok, cnt),
        scratch_shapes=[pltpu.VMEM((n_exp, LANES), F32)],
        compiler_params=_params(("arbitrary",)),
        name="out_proj_router",
    )(z, w, b, h, gate, g2, sh2, sc2, r_w_t, r_b, cnt0)


GATHER_UNROLL = 16
TOKEN_DMA_PRIORITY = 0
WEIGHT_DMA_PRIORITY = 1


def _gather_tokens(idx_ref, src_hbm, dst_ref, sem, n_tok, nch, *, unroll, priorities=(0, 1)):
    def issue(r, u):
        t = idx_ref[0, 0, r]
        src = src_hbm.at[pl.ds(pl.multiple_of(t * nch, nch), nch)]
        dst = dst_ref.at[pl.ds(pl.multiple_of(r * nch, nch), nch)]
        pltpu.make_async_copy(src, dst, sem).start(priority=priorities[u % len(priorities)])

    if unroll >= n_tok:
        for r in range(n_tok):
            issue(r, r)
        return

    def body(c, carry):
        for u in range(unroll):
            issue(c * unroll + u, u)
        return carry
    lax.fori_loop(0, n_tok // unroll, body, 0)


def _wait_tokens(src_hbm, dst_ref, sem):
    pltpu.make_async_copy(src_hbm.at[pl.ds(0, dst_ref.shape[0])], dst_ref, sem).wait()


def _moe_ffn_kernel(be_ref, nu_ref, nx_ref, first_ref, next_ref, x_hbm, w1_hbm, w2_hbm, b1_ref, b2_ref,
                    o_ref, xbuf, gsem, w1s, w2s, wsem, w1b, w2p, pair, *, layer, nch):
    b = pl.program_id(0)
    n_used = nu_ref[0]
    rows = MOE_ROWS
    slot = b % 2

    def weight_copies(e):
        return (pltpu.make_async_copy(w1_hbm.at[layer, e], w1s, wsem.at[0]),
                pltpu.make_async_copy(w2_hbm.at[layer, e], w2s, wsem.at[1]))

    @pl.when(b == 0)
    def _():
        for cp in weight_copies(be_ref[0]):
            cp.start(priority=WEIGHT_DMA_PRIORITY)
        _gather_tokens(first_ref, x_hbm, xbuf.at[0], gsem.at[0], rows, nch, unroll=GATHER_UNROLL,
                       priorities=(TOKEN_DMA_PRIORITY,))

    @pl.when(b >= n_used)
    def _():
        o_ref[...] = jnp.zeros(o_ref.shape, o_ref.dtype)

    @pl.when(b < n_used)
    def _():
        e = be_ref[b]
        changed = jnp.logical_or(b == 0, e != be_ref[jnp.maximum(b - 1, 0)])

        @pl.when(changed)
        def _():
            for cp in weight_copies(e):
                cp.wait()
            w1b[...] = w1s[...].astype(BF16)
            half = LANES // 2
            for m in range(w2p.shape[0] // LANES):
                for c in range(nch):
                    cols = slice(c * LANES, (c + 1) * LANES)
                    pair[pl.ds(c * LANES, half, stride=2), :] = w2s[m * LANES:m * LANES + half, cols]
                    pair[pl.ds(c * LANES + 1, half, stride=2), :] = w2s[m * LANES + half:(m + 1) * LANES, cols]
                    w2p[m * LANES:(m + 1) * LANES, cols] = pair[c * LANES:(c + 1) * LANES, :].astype(BF16)
            nxt = nx_ref[b]

            @pl.when(nxt >= 0)
            def _():
                for cp in weight_copies(nxt):
                    cp.start(priority=WEIGHT_DMA_PRIORITY)

        _gather_tokens(next_ref, x_hbm, xbuf.at[1 - slot], gsem.at[1 - slot], rows, nch, unroll=rows,
                       priorities=(TOKEN_DMA_PRIORITY,))
        _wait_tokens(x_hbm, xbuf.at[slot], gsem.at[slot])
        x = _load_row_tiles(xbuf.at[slot], 0, rows, nch, BF16)
        hcat = jnp.dot(x, w1b[...], preferred_element_type=F32) + b1_ref[0, 0]
        glu = jnp.minimum(hcat, SWIGLU_LIMIT)
        gact = glu * _sigmoid(SWIGLU_ALPHA * glu)
        lin = jnp.clip(hcat, -SWIGLU_LIMIT, SWIGLU_LIMIT) + 1.0
        lane = lax.broadcasted_iota(I32, (rows, LANES), 1)
        even = (lane % 2) == 0
        acts = []
        for m in range(hcat.shape[1] // (2 * LANES)):
            a0, a1, a2 = 2 * m * LANES, (2 * m + 1) * LANES, (2 * m + 2) * LANES
            pa = gact[:, a0:a1] * pltpu.roll(lin[:, a0:a1], LANES - 1, axis=1)
            pb = pltpu.roll(gact[:, a1:a2], 1, axis=1) * lin[:, a1:a2]
            acts.append(jnp.where(even, pa, pb).astype(BF16))
        act = jnp.concatenate(acts, axis=1)
        y = jnp.dot(act, w2p[...], preferred_element_type=F32) + b2_ref[0, 0]
        _store_row_tiles(o_ref, y)

        @pl.when(b == n_used - 1)
        def _():
            _wait_tokens(x_hbm, xbuf.at[1 - slot], gsem.at[1 - slot])


def _moe_ffn(block_e, n_used, next_e, src_tok, xrows, w1, b1, w2, b2, layer):
    nb = src_tok.shape[0]
    rows = src_tok.shape[2]
    depth, n_exp, d, f2 = w1.shape
    f = f2 // 2
    nch = d // LANES
    last_used = lambda b, be, nu, nx: jnp.minimum(b + 1, nu[0] - 1)
    grid_spec = pltpu.PrefetchScalarGridSpec(
        num_scalar_prefetch=3,
        grid=(nb,),
        in_specs=[
            pl.BlockSpec((1, 1, rows), lambda b, be, nu, nx: (0, 0, 0), memory_space=pltpu.SMEM),
            pl.BlockSpec((1, 1, rows), lambda b, be, nu, nx: (last_used(b, be, nu, nx), 0, 0),
                         memory_space=pltpu.SMEM),
            pl.BlockSpec(memory_space=pl.ANY),
            pl.BlockSpec(memory_space=pl.ANY),
            pl.BlockSpec(memory_space=pl.ANY),
            pl.BlockSpec((1, 1, 1, f2), lambda b, be, nu, nx: (layer, be[b], 0, 0)),
            pl.BlockSpec((1, 1, 1, d), lambda b, be, nu, nx: (layer, be[b], 0, 0)),
        ],
        out_specs=pl.BlockSpec((rows * nch, LANES), lambda b, be, nu, nx: (b, 0)),
        scratch_shapes=[pltpu.VMEM((2, rows * nch, LANES), F32),
                        pltpu.SemaphoreType.DMA((2,)),
                        pltpu.VMEM((d, f2), F32),
                        pltpu.VMEM((f, d), F32),
                        pltpu.SemaphoreType.DMA((2,)),
                        pltpu.VMEM((d, f2), BF16),
                        pltpu.VMEM((f, d), BF16),
                        pltpu.VMEM((nch * LANES, LANES), F32)],
    )
    return pl.pallas_call(
        functools.partial(_moe_ffn_kernel, layer=layer, nch=nch),
        out_shape=jax.ShapeDtypeStruct((nb * rows * nch, LANES), F32),
        grid_spec=grid_spec,
        compiler_params=_params(("arbitrary",)),
        name="moe_ffn",
    )(block_e, n_used, next_e, src_tok, src_tok, xrows, w1, w2,
      b1.reshape(depth, n_exp, 1, f2), b2.reshape(depth, n_exp, 1, d))


def _combine_kernel(first_ref, next_ref, y_hbm, gts_ref, h_ref, gate_ref, g_ref, sh_ref, sc_ref,
                    h2_ref, *rest, n_tiles, with_norm):
    if with_norm:
        hn_ref, ybuf, sem = rest
    else:
        ybuf, sem = rest
    i = pl.program_id(0)
    tm, d = h_ref.shape
    nch = d // LANES
    rows = TOP_K * tm
    slot = i % 2

    @pl.when(i == 0)
    def _():
        _gather_tokens(first_ref, y_hbm, ybuf.at[0], sem.at[0], rows, nch, unroll=GATHER_UNROLL)

    @pl.when(i + 1 < n_tiles)
    def _():
        _gather_tokens(next_ref, y_hbm, ybuf.at[1 - slot], sem.at[1 - slot], rows, nch, unroll=GATHER_UNROLL)

    _wait_tokens(y_hbm, ybuf.at[slot], sem.at[slot])
    gks = [gts_ref[:, k:k + 1] for k in range(TOP_K)]
    pieces = []
    for j in range(nch):
        acc = None
        for k in range(TOP_K):
            part = ybuf[slot, pl.ds(k * tm * nch + j, tm, stride=nch), :] * gks[k]
            acc = part if acc is None else acc + part
        pieces.append(acc)
    h2 = h_ref[...] + gate_ref[...] * jnp.concatenate(pieces, axis=1)
    h2_ref[...] = h2
    if with_norm:
        hn_ref[...] = _modulate(h2, g_ref[...], sh_ref[...], sc_ref[...]).astype(BF16)


def _moe_combine(dest, y, gts_t, h, gate, g1, sh1, sc1, *, with_norm):
    m, d = h.shape
    n_tiles = dest.shape[0]
    tm = m // n_tiles
    vec = pl.BlockSpec((1, d), lambda i: (0, 0))
    row = pl.BlockSpec((tm, d), lambda i: (i, 0))
    out_shape = [jax.ShapeDtypeStruct((m, d), F32)]
    out_specs = [row]
    if with_norm:
        out_shape.append(jax.ShapeDtypeStruct((m, d), BF16))
        out_specs.append(row)
    res = pl.pallas_call(
        functools.partial(_combine_kernel, n_tiles=n_tiles, with_norm=with_norm),
        out_shape=tuple(out_shape),
        grid=(n_tiles,),
        in_specs=[
            pl.BlockSpec((1, 1, TOP_K * tm), lambda i: (0, 0, 0), memory_space=pltpu.SMEM),
            pl.BlockSpec((1, 1, TOP_K * tm), lambda i: (jnp.minimum(i + 1, n_tiles - 1), 0, 0),
                         memory_space=pltpu.SMEM),
            pl.BlockSpec(memory_space=pl.ANY),
            pl.BlockSpec((tm, 8), lambda i: (i, 0)),
            row, vec, vec, vec, vec,
        ],
        out_specs=tuple(out_specs),
        scratch_shapes=[pltpu.VMEM((2, TOP_K * tm * (d // LANES), LANES), F32),
                        pltpu.SemaphoreType.DMA((2,))],
        compiler_params=_params(("arbitrary",)),
        name="moe_combine_norm" if with_norm else "moe_combine",
    )(dest, dest, y, gts_t, h, gate, g1, sh1, sc1)
    return res if with_norm else (res[0], None)


def _dispatch_plan(idx, rank, counts, rows):
    n_exp = counts.shape[0]
    n_tok = idx.shape[1]
    n_assign = TOP_K * n_tok
    nb = n_assign // rows + n_exp
    blocks_e = (counts + rows - 1) // rows
    blk_end = jnp.cumsum(blocks_e)
    blk_start = blk_end - blocks_e
    n_used = blk_end[-1]
    e_ids = jnp.arange(n_exp, dtype=I32)
    chosen = idx[None, :, :] == e_ids[:, None, None]
    dest = jnp.sum(jnp.where(chosen, blk_start[:, None, None], 0), axis=0) * rows + rank
    blk = jnp.minimum(jnp.arange(nb, dtype=I32), n_used - 1)
    block_e = jnp.minimum(jnp.sum((blk[:, None] >= blk_end[None, :]).astype(I32), axis=1), n_exp - 1)
    group_end = jnp.sum(jnp.where(block_e[:, None] == e_ids[None, :], blk_end[None, :], 0), axis=1)
    follower = jnp.sum(jnp.where(group_end[:, None] == jnp.arange(nb, dtype=I32)[None, :],
                                 block_e[None, :], 0), axis=1)
    next_e = jnp.where(group_end < n_used, follower, -1).astype(I32)
    tok = jnp.broadcast_to(jnp.arange(n_tok, dtype=I32)[None, :], dest.shape)
    src_tok = jnp.zeros((nb * rows,), I32).at[dest.reshape(-1)].set(tok.reshape(-1))
    return block_e.astype(I32), n_used.reshape(1).astype(I32), next_e, src_tok.reshape(nb, 1, rows), dest


def _tile_dest(dest, tm):
    n_tok = dest.shape[1]
    return dest.reshape(TOP_K, n_tok // tm, tm).transpose(1, 0, 2).reshape(n_tok // tm, 1, TOP_K * tm)


def _rope_tables(seq):
    t = jnp.arange(seq, dtype=I32)
    row_pos = (t // GRID_W).astype(F32)
    col_pos = (t % GRID_W).astype(F32)
    inv_freq = ROPE_BASE ** (-jnp.arange(ROPE_PAIRS, dtype=F32) / ROPE_PAIRS)
    ang_r = row_pos[:, None] * inv_freq[None, :]
    ang_c = col_pos[:, None] * inv_freq[None, :]
    cos = jnp.concatenate([jnp.cos(ang_r), jnp.cos(ang_r), jnp.cos(ang_c), jnp.cos(ang_c)], axis=1)
    sin = jnp.concatenate([-jnp.sin(ang_r), jnp.sin(ang_r), -jnp.sin(ang_c), jnp.sin(ang_c)], axis=1)
    return jnp.concatenate([cos, cos], axis=1), jnp.concatenate([sin, sin], axis=1)


def _head_segments(width):
    seg = jnp.arange(width, dtype=I32) // HEAD_DIM
    return (seg[:, None] == seg[None, :]).astype(BF16)


def _layer_plan(depth):
    counts = [0] * N_MIXERS
    plan = []
    for i in range(depth):
        kind = i % N_MIXERS
        plan.append((kind, counts[kind]))
        counts[kind] += 1
    return plan


def kernel(x, c, ctx, c_ctx, w_mod, b_mod, g_norm1, g_norm2, a_w_in, a_conv, a_w_out, b_w_qkv, b_b_qkv, b_g_q, b_g_k, b_sink, b_w_o, b_b_o, c_w_in, c_b_in, c_g_v, c_w_s, c_b_s, c_w_out, c_b_out, r_w, r_b, e_w1, e_b1, e_w2, e_b2):
    bsz, seq, d = x.shape
    assert bsz == 1
    depth = w_mod.shape[0]
    n_exp = r_w.shape[2]
    n_ctx = ctx.shape[1]
    plan = _layer_plan(depth)
    readers = [i for i, (kind, _) in enumerate(plan) if kind == 1]
    last_reader = readers[-1] if readers else -1
    q_width = d
    kv_width = (b_w_qkv.shape[2] - q_width) // 2
    rows = MOE_ROWS
    tm_c = 128

    cc = jnp.concatenate([c, c_ctx[None, :], jnp.zeros((6, d), F32)], axis=0)
    mod = _mod_vectors(cc, w_mod, b_mod)

    def mod_vec(layer, stream, chunk):
        return mod[layer, stream:stream + 1, chunk * d:(chunk + 1) * d]

    def vec(a):
        return a.reshape(1, -1)

    cos, sin = _rope_tables(seq)
    seg_q = _head_segments(GQA_GROUP * HEAD_DIM)
    seg_k = _head_segments(kv_width)
    zero_bias = jnp.zeros((1, d), F32)
    zero_cnt = jnp.zeros((n_exp, LANES), F32)

    h = x[0]
    hc = ctx[0]
    hn = _prenorm(h, vec(g_norm1[0]), mod_vec(0, 0, 0), mod_vec(0, 0, 1))
    hn_c = None
    if 0 <= last_reader:
        hn_c = _prenorm(hc, vec(g_norm1[0]), mod_vec(0, 1, 0), mod_vec(0, 1, 1))

    for i, (kind, s) in enumerate(plan):
        upd_ctx = i < last_reader
        read_ctx = i <= last_reader
        streams = [(0, hn, h)] + ([(1, hn_c, hc)] if upd_ctx else [])

        zs = []
        if kind == 0:
            w_in = a_w_in[s].astype(BF16)
            w_out, b_out = a_w_out[s].astype(BF16), zero_bias
            for _, hn_s, _h in streams:
                bcv = _matmul(hn_s, w_in, jnp.zeros((1, 3 * d), F32))
                zs.append(_short_conv_gate(bcv, a_conv[s]))
        elif kind == 1:
            w_qkv = b_w_qkv[s].astype(BF16)
            w_out, b_out = b_w_o[s].astype(BF16), vec(b_b_o[s])
            g_q = vec(jnp.concatenate([b_g_q[s], b_g_q[s]]))
            g_k = vec(jnp.concatenate([b_g_k[s], b_g_k[s]]))
            qkv = _matmul(hn, w_qkv, vec(b_b_qkv[s]))
            qkv_c = _matmul(hn_c, w_qkv, vec(b_b_qkv[s]))
            kd, vlr = _key_value_prep(qkv, g_k, cos, sin, seg_k, q_width=q_width, kv_width=kv_width, rope=True)
            kdc, vlrc = _key_value_prep(qkv_c, g_k, cos[:n_ctx], sin[:n_ctx], seg_k,
                                        q_width=q_width, kv_width=kv_width, rope=False)
            zs.append(_window_attention(qkv, kd, vlr, kdc, vlrc, b_sink[s], g_q, cos, sin, seg_q,
                                        q_width=q_width))
            assert not upd_ctx
        else:
            w_in = c_w_in[s].astype(BF16)
            w_out, b_out = c_w_out[s].astype(BF16), vec(c_b_out[s])
            for _, hn_s, _h in streams:
                uv = _matmul(hn_s, w_in, vec(c_b_in[s]), gelu=True)
                zs.append(_spatial_gate(uv, vec(c_g_v[s]), c_w_s[s], c_b_s[s].T))

        r_w_t = r_w[i].T.astype(BF16)
        r_b_col = jnp.broadcast_to(r_b[i][:, None], (n_exp, LANES))
        outs = []
        cnt = zero_cnt
        for (st, _hn, h_s), z in zip(streams, zs):
            res = _out_proj_router(z, w_out, b_out, h_s, mod_vec(i, st, 2), vec(g_norm2[i]),
                                   mod_vec(i, st, 3), mod_vec(i, st, 4), r_w_t, r_b_col, cnt)
            outs.append(res)
            cnt = res[5]
        if len(outs) == 1:
            h1s, xw, idx, gts, rank = [outs[0][0]], outs[0][1], outs[0][2], outs[0][3], outs[0][4]
        else:
            h1s = [o[0] for o in outs]
            xw, idx, gts, rank = [jnp.concatenate([o[j] for o in outs], axis=(0 if j == 1 else 1))
                                  for j in (1, 2, 3, 4)]
        counts = cnt[:, 0].astype(I32)

        block_e, n_used, next_e, src_tok, dest = _dispatch_plan(idx[:TOP_K], rank[:TOP_K], counts, rows)
        y = _moe_ffn(block_e, n_used, next_e, src_tok, xw, e_w1, e_b1, e_w2, e_b2, i)

        last = i == depth - 1
        new = []
        off = 0
        for (st, _hn, _h), h1 in zip(streams, h1s):
            n_tok = h1.shape[0]
            sl = slice(off, off + n_tok)
            off += n_tok
            nxt = i + 1
            need_norm = (not last) and (st == 0 or nxt <= last_reader)
            if need_norm:
                g1, sh1, sc1 = vec(g_norm1[nxt]), mod_vec(nxt, st, 0), mod_vec(nxt, st, 1)
            else:
                g1, sh1, sc1 = zero_bias, zero_bias, zero_bias
            h2, hn2 = _moe_combine(_tile_dest(dest[:, sl], tm_c), y, gts[:, sl].T,
                                   h1, mod_vec(i, st, 5), g1, sh1, sc1, with_norm=need_norm)
            new.append((h2, hn2))
        h, hn = new[0]
        if upd_ctx:
            hc, hn_c = new[1]
    return h[None]
```

```python
import functools

import jax
import jax.numpy as jnp
from jax import lax
from jax.experimental import pallas as pl
from jax.experimental.pallas import tpu as pltpu

F32 = jnp.float32
BF16 = jnp.bfloat16
U32 = jnp.uint32
I32 = jnp.int32

NORM_EPS = 1e-6
GRID_W = 64
HEAD_DIM = 64
GQA_GROUP = 8
WINDOW = 128
ATTN_BLOCK = 128
ATTN_SCALE = HEAD_DIM ** -0.5
ROPE_BASE = 10000.0
ROPE_PAIRS = HEAD_DIM // 4
NEG_INF = -1e30
LOG2_E = 1.4426950408889634
CHUNK = 128
N_SPATIAL_GROUPS = 8
TOP_K = 4
SWIGLU_LIMIT = 7.0
SWIGLU_ALPHA = 1.702
MOD_CHUNKS = 6
CONV_W = 3
N_MIXERS = 3

LANES = 128
MOE_ROWS = 256
VMEM_LIMIT = 56 * 1024 * 1024


def _params(semantics, vmem=VMEM_LIMIT):
    return pltpu.CompilerParams(dimension_semantics=semantics, vmem_limit_bytes=vmem)


def _store_row_tiles(ref, val):
    rows, d = val.shape
    nch = d // LANES
    for j in range(nch):
        ref[pl.ds(j, rows, stride=nch), :] = val[:, j * LANES:(j + 1) * LANES]


def _gather_pitch(nch):
    tiles = nch // 8
    return 8 * (tiles + 1 - tiles % 2) if nch % 8 == 0 else nch


def _load_row_tiles(ref, base, rows, nch, dtype, pitch):
    return jnp.concatenate(
        [ref[pl.ds(base * pitch + j, rows, stride=pitch), :].astype(dtype) for j in range(nch)], axis=1)


def _rms_scale(x):
    return lax.rsqrt(jnp.mean(x * x, axis=-1, keepdims=True) + NORM_EPS)


def _modulate(x, g, shift, scale):
    return (x * _rms_scale(x) * g) * (1.0 + scale) + shift


def _sigmoid(x):
    return 1.0 / (1.0 + jnp.exp(-x))


def _mod_kernel(cc_ref, w_ref, b_ref, o_ref):
    cc = cc_ref[...]
    cond = cc * _sigmoid(cc)
    o_ref[0] = jnp.dot(cond.astype(BF16), w_ref[0].astype(BF16),
                       preferred_element_type=F32) + b_ref[0]


def _mod_vectors(cc, w_mod, b_mod):
    depth, d, n = w_mod.shape
    tn = 1024
    return pl.pallas_call(
        _mod_kernel,
        out_shape=jax.ShapeDtypeStruct((depth, 8, n), F32),
        grid=(depth, n // tn),
        in_specs=[
            pl.BlockSpec((8, d), lambda l, j: (0, 0)),
            pl.BlockSpec((1, d, tn), lambda l, j: (l, 0, j)),
            pl.BlockSpec((1, 1, tn), lambda l, j: (l, 0, j)),
        ],
        out_specs=pl.BlockSpec((1, 8, tn), lambda l, j: (l, 0, j)),
        compiler_params=_params(("arbitrary", "arbitrary")),
        name="mod_vectors",
    )(cc, w_mod, b_mod.reshape(depth, 1, n))


def _prenorm_kernel(h_ref, g_ref, sh_ref, sc_ref, o_ref):
    o_ref[...] = _modulate(h_ref[...], g_ref[...], sh_ref[...], sc_ref[...]).astype(BF16)


def _prenorm(h, g, shift, scale):
    m, d = h.shape
    tm = 256
    vec = pl.BlockSpec((1, d), lambda i: (0, 0))
    return pl.pallas_call(
        _prenorm_kernel,
        out_shape=jax.ShapeDtypeStruct((m, d), BF16),
        grid=(m // tm,),
        in_specs=[pl.BlockSpec((tm, d), lambda i: (i, 0)), vec, vec, vec],
        out_specs=pl.BlockSpec((tm, d), lambda i: (i, 0)),
        compiler_params=_params(("arbitrary",)),
        name="prenorm",
    )(h, g, shift, scale)


def _mm_kernel(x_ref, w_ref, b_ref, o_ref, *, gelu):
    acc = jnp.dot(x_ref[...], w_ref[...], preferred_element_type=F32) + b_ref[...]
    if gelu:
        inner = 0.7978845608028654 * (acc + 0.044715 * (acc * acc * acc))
        acc = 0.5 * acc * (1.0 + jnp.tanh(inner))
    o_ref[...] = acc.astype(o_ref.dtype)


def _matmul(x, w, b, *, gelu=False):
    m, k = x.shape
    n = w.shape[1]
    tm = 1024 if m % 1024 == 0 else 256
    tn = next(t for t in (1024, 1280, 768, 512, 256) if n % t == 0)
    return pl.pallas_call(
        functools.partial(_mm_kernel, gelu=gelu),
        out_shape=jax.ShapeDtypeStruct((m, n), BF16),
        grid=(n // tn, m // tm),
        in_specs=[
            pl.BlockSpec((tm, k), lambda j, i: (i, 0)),
            pl.BlockSpec((k, tn), lambda j, i: (0, j)),
            pl.BlockSpec((1, tn), lambda j, i: (0, j)),
        ],
        out_specs=pl.BlockSpec((tm, tn), lambda j, i: (i, j)),
        compiler_params=_params(("arbitrary", "arbitrary")),
        name="matmul_gelu" if gelu else "matmul",
    )(x, w, b)


def _conv_kernel(b_ref, c_ref, v_ref, cp_ref, vp_ref, cn_ref, vn_ref, w_ref, o_ref, *, n_tiles):
    i = pl.program_id(0)
    tm = b_ref.shape[0]
    u = c_ref[...].astype(F32) * v_ref[...].astype(F32)
    u_prev = cp_ref[7:8, :].astype(F32) * vp_ref[7:8, :].astype(F32)
    u_next = cn_ref[0:1, :].astype(F32) * vn_ref[0:1, :].astype(F32)
    u_prev = jnp.where(i == 0, 0.0, u_prev)
    u_next = jnp.where(i == n_tiles - 1, 0.0, u_next)
    row = lax.broadcasted_iota(I32, u.shape, 0)
    below = jnp.where(row == 0, u_prev, pltpu.roll(u, 1, axis=0))
    above = jnp.where(row == tm - 1, u_next, pltpu.roll(u, tm - 1, axis=0))
    w = w_ref[...]
    y = w[0:1, :] * below + w[1:2, :] * u + w[2:3, :] * above
    o_ref[...] = (b_ref[...].astype(F32) * y).astype(BF16)


def _short_conv_gate(bcv, conv_w):
    m, n3 = bcv.shape
    d = n3 // 3
    tm = 512 if m % 512 == 0 else 256
    tc = 1024 if d % 1024 == 0 else 512
    nt, nc = m // tm, d // tc
    hb = tm // 8
    last8 = m // 8 - 1
    return pl.pallas_call(
        functools.partial(_conv_kernel, n_tiles=nt),
        out_shape=jax.ShapeDtypeStruct((m, d), BF16),
        grid=(nt, nc),
        in_specs=[
            pl.BlockSpec((tm, tc), lambda i, j: (i, j)),
            pl.BlockSpec((tm, tc), lambda i, j: (i, j + nc)),
            pl.BlockSpec((tm, tc), lambda i, j: (i, j + 2 * nc)),
            pl.BlockSpec((8, tc), lambda i, j: (jnp.maximum(i * hb - 1, 0), j + nc)),
            pl.BlockSpec((8, tc), lambda i, j: (jnp.maximum(i * hb - 1, 0), j + 2 * nc)),
            pl.BlockSpec((8, tc), lambda i, j: (jnp.minimum((i + 1) * hb, last8), j + nc)),
            pl.BlockSpec((8, tc), lambda i, j: (jnp.minimum((i + 1) * hb, last8), j + 2 * nc)),
            pl.BlockSpec((CONV_W, tc), lambda i, j: (0, j)),
        ],
        out_specs=pl.BlockSpec((tm, tc), lambda i, j: (i, j)),
        compiler_params=_params(("arbitrary", "arbitrary")),
        name="short_conv_gate",
    )(bcv, bcv, bcv, bcv, bcv, bcv, bcv, conv_w)


def _head_sumsq(x, seg_ref):
    x2 = x * x
    hi = x2.astype(BF16)
    lo = (x2 - hi.astype(F32)).astype(BF16)
    seg = seg_ref[...]
    return (jnp.dot(hi, seg, preferred_element_type=F32)
            + jnp.dot(lo, seg, preferred_element_type=F32))


def _swap16(x):
    n = x.shape[-1]
    lane = lax.broadcasted_iota(I32, x.shape, x.ndim - 1)
    first = (lane % 32) < 16
    return jnp.where(first, pltpu.roll(x, n - 16, axis=x.ndim - 1), pltpu.roll(x, 16, axis=x.ndim - 1))


def _tile_lanes(x, width):
    return jnp.concatenate([x] * (width // x.shape[-1]), axis=-1)


def _kprep_kernel(k_ref, v_ref, g_ref, cos_ref, sin_ref, seg_ref, kd_ref, vlr_ref, *, rope):
    k = k_ref[...].astype(F32)
    width = k.shape[-1]
    ss = _head_sumsq(k, seg_ref)
    kn = k * lax.rsqrt(ss * (1.0 / HEAD_DIM) + NORM_EPS) * _tile_lanes(g_ref[...], width)
    if rope:
        kn = kn * _tile_lanes(cos_ref[...], width) + _swap16(kn) * _tile_lanes(sin_ref[...], width)
    v = v_ref[...].astype(F32)
    lane = lax.broadcasted_iota(I32, (k.shape[0], LANES), 1)
    left = lane < HEAD_DIM
    for p in range(width // LANES):
        kp = kn[:, p * LANES:(p + 1) * LANES]
        kr = pltpu.roll(kp, HEAD_DIM, axis=1)
        kd_ref[:, (2 * p) * LANES:(2 * p + 1) * LANES] = jnp.where(left, kp, kr).astype(BF16)
        kd_ref[:, (2 * p + 1) * LANES:(2 * p + 2) * LANES] = jnp.where(left, kr, kp).astype(BF16)
        vp = v[:, p * LANES:(p + 1) * LANES]
        vr = pltpu.roll(vp, HEAD_DIM, axis=1)
        base = 4 * p * LANES
        vlr_ref[:, base:base + LANES] = jnp.where(left, vp, 0.0).astype(BF16)
        vlr_ref[:, base + LANES:base + 2 * LANES] = jnp.where(left, 0.0, vr).astype(BF16)
        vlr_ref[:, base + 2 * LANES:base + 3 * LANES] = jnp.where(left, vr, 0.0).astype(BF16)
        vlr_ref[:, base + 3 * LANES:base + 4 * LANES] = jnp.where(left, 0.0, vp).astype(BF16)


def _key_value_prep(qkv, g_k, cos, sin, seg, *, q_width, kv_width, rope):
    m = qkv.shape[0]
    tm = 256
    n_kv = kv_width // HEAD_DIM
    kb = q_width // kv_width
    return pl.pallas_call(
        functools.partial(_kprep_kernel, rope=rope),
        out_shape=(jax.ShapeDtypeStruct((m, n_kv * LANES), BF16),
                   jax.ShapeDtypeStruct((m, n_kv * 2 * LANES), BF16)),
        grid=(m // tm,),
        in_specs=[
            pl.BlockSpec((tm, kv_width), lambda i: (i, kb)),
            pl.BlockSpec((tm, kv_width), lambda i: (i, kb + 1)),
            pl.BlockSpec((1, LANES), lambda i: (0, 0)),
            pl.BlockSpec((tm, LANES), lambda i: (i, 0)),
            pl.BlockSpec((tm, LANES), lambda i: (i, 0)),
            pl.BlockSpec((kv_width, kv_width), lambda i: (0, 0)),
        ],
        out_specs=(pl.BlockSpec((tm, n_kv * LANES), lambda i: (i, 0)),
                   pl.BlockSpec((tm, n_kv * 2 * LANES), lambda i: (i, 0))),
        compiler_params=_params(("arbitrary",)),
        name="key_value_prep_rope" if rope else "key_value_prep",
    )(qkv, qkv, g_k, cos, sin, seg)


def _attn_kernel(sink_ref, q_ref, cos_ref, sin_ref, g_ref, seg_ref,
                 kd0_ref, kd1_ref, kd2_ref, vl0_ref, vl1_ref, vl2_ref, kdc_ref, vlc_ref,
                 o_ref, s_scr, p_scr, *, seq, n_kv):
    n = pl.program_id(0)
    blk = ATTN_BLOCK
    n_band = 3 * blk
    n_keys = s_scr.shape[1]
    gw = GQA_GROUP * HEAD_DIM
    r = lax.broadcasted_iota(I32, (blk, 3 * blk), 0)
    kk = lax.broadcasted_iota(I32, (blk, 3 * blk), 1)
    off = kk - r
    key_pos = n * blk + kk - blk
    mask = ((off >= blk - WINDOW) & (off <= blk + WINDOW) & (key_pos >= 0) & (key_pos < seq))
    lane = lax.broadcasted_iota(I32, (blk, LANES), 1)
    left = lane < HEAD_DIM
    cos = _tile_lanes(cos_ref[...], gw)
    sin = _tile_lanes(sin_ref[...], gw)
    gq = _tile_lanes(g_ref[...], gw)
    nt = (((1,), (1,)), ((), ()))
    for h in range(n_kv):
        q = q_ref[:, h * gw:(h + 1) * gw].astype(F32)
        ss = _head_sumsq(q, seg_ref)
        qn = q * lax.rsqrt(ss * (1.0 / HEAD_DIM) + NORM_EPS) * gq
        qn = ((qn * cos + _swap16(qn) * sin) * (ATTN_SCALE * LOG2_E)).astype(BF16)
        kd = jnp.concatenate([ref[:, h * LANES:(h + 1) * LANES] for ref in (kd0_ref, kd1_ref, kd2_ref)], axis=0)
        kdc = kdc_ref[:, h * LANES:(h + 1) * LANES]
        vb = 2 * h * LANES
        vl = jnp.concatenate([ref[:, vb:vb + LANES] for ref in (vl0_ref, vl1_ref, vl2_ref)], axis=0)
        vr = jnp.concatenate([ref[:, vb + LANES:vb + 2 * LANES] for ref in (vl0_ref, vl1_ref, vl2_ref)], axis=0)
        vlc = vlc_ref[:, vb:vb + LANES]
        vrc = vlc_ref[:, vb + LANES:vb + 2 * LANES]

        for g in range(GQA_GROUP):
            pair = qn[:, (g // 2) * LANES:(g // 2 + 1) * LANES]
            qg = jnp.where(left, pair, 0.0) if g % 2 == 0 else jnp.where(left, 0.0, pair)
            s = lax.dot_general(qg, kd, nt, preferred_element_type=F32)
            s_scr[g * blk:(g + 1) * blk, 0:n_band] = jnp.where(mask, s, NEG_INF)
            s_scr[g * blk:(g + 1) * blk, n_band:n_keys] = lax.dot_general(qg, kdc, nt, preferred_element_type=F32)

        sk = jnp.concatenate([jnp.full((blk, 1), sink_ref[h * GQA_GROUP + g] * LOG2_E, F32)
                              for g in range(GQA_GROUP)], axis=0)
        mx = jnp.maximum(jnp.max(s_scr[...], axis=-1, keepdims=True), sk)
        e = jnp.exp2(s_scr[...] - mx)
        s_scr[...] = e
        inv = 1.0 / (jnp.sum(e, axis=-1, keepdims=True) + jnp.exp2(sk - mx))
        p_scr[...] = (s_scr[...] * inv).astype(BF16)

        for p in range(GQA_GROUP // 2):
            r0, r1, r2 = 2 * p * blk, (2 * p + 1) * blk, (2 * p + 2) * blk
            acc = (jnp.dot(p_scr[r0:r1, 0:n_band], vl, preferred_element_type=F32)
                   + jnp.dot(p_scr[r0:r1, n_band:n_keys], vlc, preferred_element_type=F32)
                   + jnp.dot(p_scr[r1:r2, 0:n_band], vr, preferred_element_type=F32)
                   + jnp.dot(p_scr[r1:r2, n_band:n_keys], vrc, preferred_element_type=F32))
            o_ref[:, h * gw + p * LANES:h * gw + (p + 1) * LANES] = acc.astype(BF16)


def _window_attention(qkv, kd, vlr, kdc, vlrc, sink, g_q, cos, sin, seg, *, q_width):
    seq = qkv.shape[0]
    n_kv = kd.shape[1] // LANES
    nb = seq // ATTN_BLOCK
    n_ctx = kdc.shape[0]
    gw = GQA_GROUP * HEAD_DIM

    def band(shift, width):
        return pl.BlockSpec((ATTN_BLOCK, width),
                            lambda n, s: (jnp.clip(n + shift, 0, nb - 1), 0))

    grid_spec = pltpu.PrefetchScalarGridSpec(
        num_scalar_prefetch=1,
        grid=(nb,),
        in_specs=[
            pl.BlockSpec((ATTN_BLOCK, q_width), lambda n, s: (n, 0)),
            pl.BlockSpec((ATTN_BLOCK, LANES), lambda n, s: (n, 0)),
            pl.BlockSpec((ATTN_BLOCK, LANES), lambda n, s: (n, 0)),
            pl.BlockSpec((1, LANES), lambda n, s: (0, 0)),
            pl.BlockSpec((gw, gw), lambda n, s: (0, 0)),
            band(-1, kd.shape[1]), band(0, kd.shape[1]), band(1, kd.shape[1]),
            band(-1, vlr.shape[1]), band(0, vlr.shape[1]), band(1, vlr.shape[1]),
            pl.BlockSpec((n_ctx, kdc.shape[1]), lambda n, s: (0, 0)),
            pl.BlockSpec((n_ctx, vlrc.shape[1]), lambda n, s: (0, 0)),
        ],
        out_specs=pl.BlockSpec((ATTN_BLOCK, q_width), lambda n, s: (n, 0)),
        scratch_shapes=[pltpu.VMEM((GQA_GROUP * ATTN_BLOCK, 3 * ATTN_BLOCK + n_ctx), F32),
                        pltpu.VMEM((GQA_GROUP * ATTN_BLOCK, 3 * ATTN_BLOCK + n_ctx), BF16)],
    )
    return pl.pallas_call(
        functools.partial(_attn_kernel, seq=seq, n_kv=n_kv),
        out_shape=jax.ShapeDtypeStruct((seq, q_width), BF16),
        grid_spec=grid_spec,
        compiler_params=_params(("arbitrary",)),
        name="window_attention",
    )(sink, qkv, cos, sin, g_q, seg, kd, kd, kd, vlr, vlr, vlr, kdc, vlrc)


def _sgu_kernel(u_ref, v_ref, g_ref, ws_ref, bs_ref, o_ref):
    v = v_ref[...].astype(F32)
    vn = (v * _rms_scale(v) * g_ref[...]).astype(BF16)
    d = v.shape[-1]
    gwid = d // N_SPATIAL_GROUPS
    for g in range(N_SPATIAL_GROUPS):
        cols = slice(g * gwid, (g + 1) * gwid)
        vs = jnp.dot(ws_ref[g].astype(BF16), vn[:, cols], preferred_element_type=F32) + bs_ref[:, g:g + 1]
        o_ref[:, cols] = (u_ref[:, cols].astype(F32) * vs).astype(BF16)


def _spatial_gate(uv, g_v, w_s, b_s_t):
    m, d2 = uv.shape
    d = d2 // 2
    return pl.pallas_call(
        _sgu_kernel,
        out_shape=jax.ShapeDtypeStruct((m, d), BF16),
        grid=(m // CHUNK,),
        in_specs=[
            pl.BlockSpec((CHUNK, d), lambda i: (i, 0)),
            pl.BlockSpec((CHUNK, d), lambda i: (i, 1)),
            pl.BlockSpec((1, d), lambda i: (0, 0)),
            pl.BlockSpec((N_SPATIAL_GROUPS, CHUNK, CHUNK), lambda i: (0, 0, 0)),
            pl.BlockSpec((CHUNK, N_SPATIAL_GROUPS), lambda i: (0, 0)),
        ],
        out_specs=pl.BlockSpec((CHUNK, d), lambda i: (i, 0)),
        compiler_params=_params(("arbitrary",)),
        name="spatial_gate",
    )(uv, uv, g_v, w_s, b_s_t)


def _out_router_kernel(z_ref, w_ref, b_ref, h_ref, gate_ref, g_ref, sh_ref, sc_ref,
                       rw_ref, rb_ref, cnt0_ref,
                       h1_ref, xw_ref, idx_ref, gts_ref, rank_ref, cnt_ref, run_ref):
    i = pl.program_id(0)
    tm, d = h_ref.shape
    n_exp = rw_ref.shape[0]

    @pl.when(i == 0)
    def _():
        run_ref[...] = cnt0_ref[...]

    out = jnp.dot(z_ref[...], w_ref[...], preferred_element_type=F32) + b_ref[...]
    h1 = h_ref[...] + gate_ref[...] * out
    h1_ref[...] = h1
    y = _modulate(h1, g_ref[...], sh_ref[...], sc_ref[...])
    yb = y.astype(BF16)
    _store_row_tiles(xw_ref, y)

    logits = lax.dot_general(rw_ref[...], yb, (((1,), (1,)), ((), ())),
                             preferred_element_type=F32) + rb_ref[:, 0:1]
    e_iota = lax.broadcasted_iota(I32, (n_exp, tm), 0).astype(F32)
    vals, idxs = [], []
    cur = logits
    for _k in range(TOP_K):
        mx = jnp.max(cur, axis=0, keepdims=True)
        ix = jnp.min(jnp.where(cur == mx, e_iota, float(n_exp)), axis=0, keepdims=True)
        vals.append(mx)
        idxs.append(ix)
        cur = jnp.where(e_iota == ix, -jnp.inf, cur)
    exps = [jnp.exp(v - vals[0]) for v in vals]
    den = exps[0] + exps[1] + exps[2] + exps[3]
    sel = jnp.zeros((n_exp, tm), F32)
    for ix in idxs:
        sel = sel + jnp.where(e_iota == ix, 1.0, 0.0)
    before = (lax.broadcasted_iota(I32, (tm, tm), 0) < lax.broadcasted_iota(I32, (tm, tm), 1))
    prior = jnp.dot(sel.astype(BF16), jnp.where(before, 1.0, 0.0).astype(BF16),
                    preferred_element_type=F32) + run_ref[:, 0:1]
    zeros = jnp.zeros((8 - TOP_K, tm), F32)
    ranks = [jnp.sum(jnp.where(e_iota == ix, prior, 0.0), axis=0, keepdims=True) for ix in idxs]
    idx_ref[...] = jnp.concatenate(idxs + [zeros], axis=0).astype(I32)
    gts_ref[...] = jnp.concatenate([e / den for e in exps] + [zeros], axis=0)
    rank_ref[...] = jnp.concatenate(ranks + [zeros], axis=0).astype(I32)
    run_ref[...] = run_ref[...] + jnp.sum(sel, axis=1, keepdims=True)
    cnt_ref[...] = run_ref[...]


def _out_proj_router(z, w, b, h, gate, g2, sh2, sc2, r_w_t, r_b, cnt0):
    m, d = h.shape
    k = z.shape[1]
    n_exp = r_w_t.shape[0]
    tm = 256
    nch = d // LANES
    vec = pl.BlockSpec((1, d), lambda i: (0, 0))
    row = lambda width: pl.BlockSpec((tm, width), lambda i: (i, 0))
    tok = pl.BlockSpec((8, tm), lambda i: (0, i))
    cnt = pl.BlockSpec((n_exp, LANES), lambda i: (0, 0))
    return pl.pallas_call(
        _out_router_kernel,
        out_shape=(jax.ShapeDtypeStruct((m, d), F32),
                   jax.ShapeDtypeStruct((m * nch, LANES), F32),
                   jax.ShapeDtypeStruct((8, m), I32),
                   jax.ShapeDtypeStruct((8, m), F32),
                   jax.ShapeDtypeStruct((8, m), I32),
                   jax.ShapeDtypeStruct((n_exp, LANES), F32)),
        grid=(m // tm,),
        in_specs=[row(k), pl.BlockSpec((k, d), lambda i: (0, 0)), vec, row(d), vec, vec, vec, vec,
                  pl.BlockSpec((n_exp, d), lambda i: (0, 0)), cnt, cnt],
        out_specs=(row(d), pl.BlockSpec((tm * nch, LANES), lambda i: (i, 0)), tok, tok, tok, cnt),
        scratch_shapes=[pltpu.VMEM((n_exp, LANES), F32)],
        compiler_params=_params(("arbitrary",)),
        name="out_proj_router",
    )(z, w, b, h, gate, g2, sh2, sc2, r_w_t, r_b, cnt0)


GATHER_UNROLL = 16
WEIGHT_CHUNKS = 4


def _gather_tokens(idx_ref, src_hbm, dst_ref, sem, n_tok, nch, *, unroll, priorities=(0, 1)):
    pitch = _gather_pitch(nch)

    def issue(r, u):
        t = idx_ref[0, 0, r]
        src = src_hbm.at[pl.ds(pl.multiple_of(t * nch, 8), nch)]
        dst = dst_ref.at[pl.ds(pl.multiple_of(r * pitch, 8), nch)]
        pltpu.make_async_copy(src, dst, sem).start(priority=priorities[u % len(priorities)])

    if unroll >= n_tok:
        for r in range(n_tok):
            issue(r, r)
        return

    def body(c, carry):
        for u in range(unroll):
            issue(c * unroll + u, u)
        return carry
    lax.fori_loop(0, n_tok // unroll, body, 0)


def _wait_tokens(src_hbm, dst_ref, sem, n_tok, nch):
    n_rows = n_tok * nch
    pltpu.make_async_copy(src_hbm.at[pl.ds(0, n_rows)], dst_ref.at[pl.ds(0, n_rows)], sem).wait()


def _moe_ffn_kernel(be_ref, nu_ref, nx_ref, first_ref, next_ref, x_hbm, w1_hbm, w2_hbm, b1_ref, b2_ref,
                    o_ref, xbuf, gsem, w1s, w2s, wsem, w1b, w2p, pair, *, layer, nch):
    b = pl.program_id(0)
    n_used = nu_ref[0]
    rows = MOE_ROWS
    slot = b % 2

    def chunk_copies(w_hbm, stage, sem, e):
        n = stage.shape[0] // WEIGHT_CHUNKS
        return [pltpu.make_async_copy(w_hbm.at[layer, e, pl.ds(i * n, n)], stage.at[pl.ds(i * n, n)], sem)
                for i in range(WEIGHT_CHUNKS)]

    def start_all(copies):
        for i, cp in enumerate(copies):
            cp.start(priority=i % 2)

    def wait_all(copies):
        for cp in copies:
            cp.wait()

    @pl.when(b == 0)
    def _():
        start_all(chunk_copies(w1_hbm, w1s, wsem.at[0], be_ref[0]))
        start_all(chunk_copies(w2_hbm, w2s, wsem.at[1], be_ref[0]))
        _gather_tokens(first_ref, x_hbm, xbuf.at[0], gsem.at[0], rows, nch, unroll=GATHER_UNROLL)

    @pl.when(b >= n_used)
    def _():
        o_ref[...] = jnp.zeros(o_ref.shape, o_ref.dtype)

    @pl.when(b < n_used)
    def _():
        e = be_ref[b]
        changed = jnp.logical_or(b == 0, e != be_ref[jnp.maximum(b - 1, 0)])

        @pl.when(changed)
        def _():
            nxt = nx_ref[b]
            wait_all(chunk_copies(w1_hbm, w1s, wsem.at[0], e))
            w1b[...] = w1s[...].astype(BF16)

            @pl.when(nxt >= 0)
            def _():
                start_all(chunk_copies(w1_hbm, w1s, wsem.at[0], nxt))

            wait_all(chunk_copies(w2_hbm, w2s, wsem.at[1], e))
            half = LANES // 2
            for m in range(w2p.shape[0] // LANES):
                for c in range(nch):
                    cols = slice(c * LANES, (c + 1) * LANES)
                    pair[pl.ds(c * LANES, half, stride=2), :] = w2s[m * LANES:m * LANES + half, cols]
                    pair[pl.ds(c * LANES + 1, half, stride=2), :] = w2s[m * LANES + half:(m + 1) * LANES, cols]
                    w2p[m * LANES:(m + 1) * LANES, cols] = pair[c * LANES:(c + 1) * LANES, :].astype(BF16)

            @pl.when(nxt >= 0)
            def _():
                start_all(chunk_copies(w2_hbm, w2s, wsem.at[1], nxt))

        _gather_tokens(next_ref, x_hbm, xbuf.at[1 - slot], gsem.at[1 - slot], rows, nch, unroll=rows)
        _wait_tokens(x_hbm, xbuf.at[slot], gsem.at[slot], rows, nch)
        x = _load_row_tiles(xbuf.at[slot], 0, rows, nch, BF16, _gather_pitch(nch))
        hcat = jnp.dot(x, w1b[...], preferred_element_type=F32) + b1_ref[0, 0]
        glu = jnp.minimum(hcat, SWIGLU_LIMIT)
        gact = glu * _sigmoid(SWIGLU_ALPHA * glu)
        lin = jnp.clip(hcat, -SWIGLU_LIMIT, SWIGLU_LIMIT) + 1.0
        lane = lax.broadcasted_iota(I32, (rows, LANES), 1)
        even = (lane % 2) == 0
        acts = []
        for m in range(hcat.shape[1] // (2 * LANES)):
            a0, a1, a2 = 2 * m * LANES, (2 * m + 1) * LANES, (2 * m + 2) * LANES
            pa = gact[:, a0:a1] * pltpu.roll(lin[:, a0:a1], LANES - 1, axis=1)
            pb = pltpu.roll(gact[:, a1:a2], 1, axis=1) * lin[:, a1:a2]
            acts.append(jnp.where(even, pa, pb).astype(BF16))
        act = jnp.concatenate(acts, axis=1)
        y = jnp.dot(act, w2p[...], preferred_element_type=F32) + b2_ref[0, 0]
        _store_row_tiles(o_ref, y)

        @pl.when(b == n_used - 1)
        def _():
            _wait_tokens(x_hbm, xbuf.at[1 - slot], gsem.at[1 - slot], rows, nch)


def _moe_ffn(block_e, n_used, next_e, src_tok, xrows, w1, b1, w2, b2, layer):
    nb = src_tok.shape[0]
    rows = src_tok.shape[2]
    depth, n_exp, d, f2 = w1.shape
    f = f2 // 2
    nch = d // LANES
    last_used = lambda b, be, nu, nx: jnp.minimum(b + 1, nu[0] - 1)
    grid_spec = pltpu.PrefetchScalarGridSpec(
        num_scalar_prefetch=3,
        grid=(nb,),
        in_specs=[
            pl.BlockSpec((1, 1, rows), lambda b, be, nu, nx: (0, 0, 0), memory_space=pltpu.SMEM),
            pl.BlockSpec((1, 1, rows), lambda b, be, nu, nx: (last_used(b, be, nu, nx), 0, 0),
                         memory_space=pltpu.SMEM),
            pl.BlockSpec(memory_space=pl.ANY),
            pl.BlockSpec(memory_space=pl.ANY),
            pl.BlockSpec(memory_space=pl.ANY),
            pl.BlockSpec((1, 1, 1, f2), lambda b, be, nu, nx: (layer, be[b], 0, 0)),
            pl.BlockSpec((1, 1, 1, d), lambda b, be, nu, nx: (layer, be[b], 0, 0)),
        ],
        out_specs=pl.BlockSpec((rows * nch, LANES), lambda b, be, nu, nx: (b, 0)),
        scratch_shapes=[pltpu.VMEM((2, rows * _gather_pitch(nch), LANES), F32),
                        pltpu.SemaphoreType.DMA((2,)),
                        pltpu.VMEM((d, f2), F32),
                        pltpu.VMEM((f, d), F32),
                        pltpu.SemaphoreType.DMA((2,)),
                        pltpu.VMEM((d, f2), BF16),
                        pltpu.VMEM((f, d), BF16),
                        pltpu.VMEM((nch * LANES, LANES), F32)],
    )
    return pl.pallas_call(
        functools.partial(_moe_ffn_kernel, layer=layer, nch=nch),
        out_shape=jax.ShapeDtypeStruct((nb * rows * nch, LANES), F32),
        grid_spec=grid_spec,
        compiler_params=_params(("arbitrary",)),
        name="moe_ffn",
    )(block_e, n_used, next_e, src_tok, src_tok, xrows, w1, w2,
      b1.reshape(depth, n_exp, 1, f2), b2.reshape(depth, n_exp, 1, d))


def _combine_kernel(first_ref, next_ref, y_hbm, gts_ref, h_ref, gate_ref, g_ref, sh_ref, sc_ref,
                    h2_ref, *rest, n_tiles, with_norm):
    if with_norm:
        hn_ref, ybuf, sem = rest
    else:
        ybuf, sem = rest
    i = pl.program_id(0)
    tm, d = h_ref.shape
    nch = d // LANES
    rows = TOP_K * tm
    slot = i % 2

    @pl.when(i == 0)
    def _():
        _gather_tokens(first_ref, y_hbm, ybuf.at[0], sem.at[0], rows, nch, unroll=GATHER_UNROLL)

    @pl.when(i + 1 < n_tiles)
    def _():
        _gather_tokens(next_ref, y_hbm, ybuf.at[1 - slot], sem.at[1 - slot], rows, nch, unroll=GATHER_UNROLL)

    _wait_tokens(y_hbm, ybuf.at[slot], sem.at[slot], rows, nch)
    pitch = _gather_pitch(nch)
    gks = [gts_ref[:, k:k + 1] for k in range(TOP_K)]
    pieces = []
    for j in range(nch):
        acc = None
        for k in range(TOP_K):
            part = ybuf[slot, pl.ds(k * tm * pitch + j, tm, stride=pitch), :] * gks[k]
            acc = part if acc is None else acc + part
        pieces.append(acc)
    h2 = h_ref[...] + gate_ref[...] * jnp.concatenate(pieces, axis=1)
    h2_ref[...] = h2
    if with_norm:
        hn_ref[...] = _modulate(h2, g_ref[...], sh_ref[...], sc_ref[...]).astype(BF16)


def _moe_combine(dest, y, gts_t, h, gate, g1, sh1, sc1, *, with_norm):
    m, d = h.shape
    n_tiles = dest.shape[0]
    tm = m // n_tiles
    vec = pl.BlockSpec((1, d), lambda i: (0, 0))
    row = pl.BlockSpec((tm, d), lambda i: (i, 0))
    out_shape = [jax.ShapeDtypeStruct((m, d), F32)]
    out_specs = [row]
    if with_norm:
        out_shape.append(jax.ShapeDtypeStruct((m, d), BF16))
        out_specs.append(row)
    res = pl.pallas_call(
        functools.partial(_combine_kernel, n_tiles=n_tiles, with_norm=with_norm),
        out_shape=tuple(out_shape),
        grid=(n_tiles,),
        in_specs=[
            pl.BlockSpec((1, 1, TOP_K * tm), lambda i: (0, 0, 0), memory_space=pltpu.SMEM),
            pl.BlockSpec((1, 1, TOP_K * tm), lambda i: (jnp.minimum(i + 1, n_tiles - 1), 0, 0),
                         memory_space=pltpu.SMEM),
            pl.BlockSpec(memory_space=pl.ANY),
            pl.BlockSpec((tm, 8), lambda i: (i, 0)),
            row, vec, vec, vec, vec,
        ],
        out_specs=tuple(out_specs),
        scratch_shapes=[pltpu.VMEM((2, TOP_K * tm * _gather_pitch(d // LANES), LANES), F32),
                        pltpu.SemaphoreType.DMA((2,))],
        compiler_params=_params(("arbitrary",)),
        name="moe_combine_norm" if with_norm else "moe_combine",
    )(dest, dest, y, gts_t, h, gate, g1, sh1, sc1)
    return res if with_norm else (res[0], None)


def _dispatch_plan(idx, rank, counts, rows):
    n_exp = counts.shape[0]
    n_tok = idx.shape[1]
    n_assign = TOP_K * n_tok
    nb = n_assign // rows + n_exp
    blocks_e = (counts + rows - 1) // rows
    blk_end = jnp.cumsum(blocks_e)
    blk_start = blk_end - blocks_e
    n_used = blk_end[-1]
    e_ids = jnp.arange(n_exp, dtype=I32)
    chosen = idx[None, :, :] == e_ids[:, None, None]
    dest = jnp.sum(jnp.where(chosen, blk_start[:, None, None], 0), axis=0) * rows + rank
    blk = jnp.minimum(jnp.arange(nb, dtype=I32), n_used - 1)
    block_e = jnp.minimum(jnp.sum((blk[:, None] >= blk_end[None, :]).astype(I32), axis=1), n_exp - 1)
    group_end = jnp.sum(jnp.where(block_e[:, None] == e_ids[None, :], blk_end[None, :], 0), axis=1)
    follower = jnp.sum(jnp.where(group_end[:, None] == jnp.arange(nb, dtype=I32)[None, :],
                                 block_e[None, :], 0), axis=1)
    next_e = jnp.where(group_end < n_used, follower, -1).astype(I32)
    tok = jnp.broadcast_to(jnp.arange(n_tok, dtype=I32)[None, :], dest.shape)
    src_tok = jnp.zeros((nb * rows,), I32).at[dest.reshape(-1)].set(tok.reshape(-1))
    return block_e.astype(I32), n_used.reshape(1).astype(I32), next_e, src_tok.reshape(nb, 1, rows), dest


def _tile_dest(dest, tm):
    n_tok = dest.shape[1]
    return dest.reshape(TOP_K, n_tok // tm, tm).transpose(1, 0, 2).reshape(n_tok // tm, 1, TOP_K * tm)


def _rope_tables(seq):
    t = jnp.arange(seq, dtype=I32)
    row_pos = (t // GRID_W).astype(F32)
    col_pos = (t % GRID_W).astype(F32)
    inv_freq = ROPE_BASE ** (-jnp.arange(ROPE_PAIRS, dtype=F32) / ROPE_PAIRS)
    ang_r = row_pos[:, None] * inv_freq[None, :]
    ang_c = col_pos[:, None] * inv_freq[None, :]
    cos = jnp.concatenate([jnp.cos(ang_r), jnp.cos(ang_r), jnp.cos(ang_c), jnp.cos(ang_c)], axis=1)
    sin = jnp.concatenate([-jnp.sin(ang_r), jnp.sin(ang_r), -jnp.sin(ang_c), jnp.sin(ang_c)], axis=1)
    return jnp.concatenate([cos, cos], axis=1), jnp.concatenate([sin, sin], axis=1)


def _head_segments(width):
    seg = jnp.arange(width, dtype=I32) // HEAD_DIM
    return (seg[:, None] == seg[None, :]).astype(BF16)


def _layer_plan(depth):
    counts = [0] * N_MIXERS
    plan = []
    for i in range(depth):
        kind = i % N_MIXERS
        plan.append((kind, counts[kind]))
        counts[kind] += 1
    return plan


def kernel(x, c, ctx, c_ctx, w_mod, b_mod, g_norm1, g_norm2, a_w_in, a_conv, a_w_out, b_w_qkv, b_b_qkv, b_g_q, b_g_k, b_sink, b_w_o, b_b_o, c_w_in, c_b_in, c_g_v, c_w_s, c_b_s, c_w_out, c_b_out, r_w, r_b, e_w1, e_b1, e_w2, e_b2):
    bsz, seq, d = x.shape
    assert bsz == 1
    depth = w_mod.shape[0]
    n_exp = r_w.shape[2]
    n_ctx = ctx.shape[1]
    plan = _layer_plan(depth)
    readers = [i for i, (kind, _) in enumerate(plan) if kind == 1]
    last_reader = readers[-1] if readers else -1
    q_width = d
    kv_width = (b_w_qkv.shape[2] - q_width) // 2
    rows = MOE_ROWS
    tm_c = 128

    cc = jnp.concatenate([c, c_ctx[None, :], jnp.zeros((6, d), F32)], axis=0)
    mod = _mod_vectors(cc, w_mod, b_mod)

    def mod_vec(layer, stream, chunk):
        return mod[layer, stream:stream + 1, chunk * d:(chunk + 1) * d]

    def vec(a):
        return a.reshape(1, -1)

    cos, sin = _rope_tables(seq)
    seg_q = _head_segments(GQA_GROUP * HEAD_DIM)
    seg_k = _head_segments(kv_width)
    zero_bias = jnp.zeros((1, d), F32)
    zero_cnt = jnp.zeros((n_exp, LANES), F32)

    h = x[0]
    hc = ctx[0]
    hn = _prenorm(h, vec(g_norm1[0]), mod_vec(0, 0, 0), mod_vec(0, 0, 1))
    hn_c = None
    if 0 <= last_reader:
        hn_c = _prenorm(hc, vec(g_norm1[0]), mod_vec(0, 1, 0), mod_vec(0, 1, 1))

    for i, (kind, s) in enumerate(plan):
        upd_ctx = i < last_reader
        read_ctx = i <= last_reader
        streams = [(0, hn, h)] + ([(1, hn_c, hc)] if upd_ctx else [])

        zs = []
        if kind == 0:
            w_in = a_w_in[s].astype(BF16)
            w_out, b_out = a_w_out[s].astype(BF16), zero_bias
            for _, hn_s, _h in streams:
                bcv = _matmul(hn_s, w_in, jnp.zeros((1, 3 * d), F32))
                zs.append(_short_conv_gate(bcv, a_conv[s]))
        elif kind == 1:
            w_qkv = b_w_qkv[s].astype(BF16)
            w_out, b_out = b_w_o[s].astype(BF16), vec(b_b_o[s])
            g_q = vec(jnp.concatenate([b_g_q[s], b_g_q[s]]))
            g_k = vec(jnp.concatenate([b_g_k[s], b_g_k[s]]))
            qkv = _matmul(hn, w_qkv, vec(b_b_qkv[s]))
            qkv_c = _matmul(hn_c, w_qkv, vec(b_b_qkv[s]))
            kd, vlr = _key_value_prep(qkv, g_k, cos, sin, seg_k, q_width=q_width, kv_width=kv_width, rope=True)
            kdc, vlrc = _key_value_prep(qkv_c, g_k, cos[:n_ctx], sin[:n_ctx], seg_k,
                                        q_width=q_width, kv_width=kv_width, rope=False)
            zs.append(_window_attention(qkv, kd, vlr, kdc, vlrc, b_sink[s], g_q, cos, sin, seg_q,
                                        q_width=q_width))
            assert not upd_ctx
        else:
            w_in = c_w_in[s].astype(BF16)
            w_out, b_out = c_w_out[s].astype(BF16), vec(c_b_out[s])
            for _, hn_s, _h in streams:
                uv = _matmul(hn_s, w_in, vec(c_b_in[s]), gelu=True)
                zs.append(_spatial_gate(uv, vec(c_g_v[s]), c_w_s[s], c_b_s[s].T))

        r_w_t = r_w[i].T.astype(BF16)
        r_b_col = jnp.broadcast_to(r_b[i][:, None], (n_exp, LANES))
        outs = []
        cnt = zero_cnt
        for (st, _hn, h_s), z in zip(streams, zs):
            res = _out_proj_router(z, w_out, b_out, h_s, mod_vec(i, st, 2), vec(g_norm2[i]),
                                   mod_vec(i, st, 3), mod_vec(i, st, 4), r_w_t, r_b_col, cnt)
            outs.append(res)
            cnt = res[5]
        if len(outs) == 1:
            h1s, xw, idx, gts, rank = [outs[0][0]], outs[0][1], outs[0][2], outs[0][3], outs[0][4]
        else:
            h1s = [o[0] for o in outs]
            xw, idx, gts, rank = [jnp.concatenate([o[j] for o in outs], axis=(0 if j == 1 else 1))
                                  for j in (1, 2, 3, 4)]
        counts = cnt[:, 0].astype(I32)

        block_e, n_used, next_e, src_tok, dest = _dispatch_plan(idx[:TOP_K], rank[:TOP_K], counts, rows)
        y = _moe_ffn(block_e, n_used, next_e, src_tok, xw, e_w1, e_b1, e_w2, e_b2, i)

        last = i == depth - 1
        new = []
        off = 0
        for (st, _hn, _h), h1 in zip(streams, h1s):
            n_tok = h1.shape[0]
            sl = slice(off, off + n_tok)
            off += n_tok
            nxt = i + 1
            need_norm = (not last) and (st == 0 or nxt <= last_reader)
            if need_norm:
                g1, sh1, sc1 = vec(g_norm1[nxt]), mod_vec(nxt, st, 0), mod_vec(nxt, st, 1)
            else:
                g1, sh1, sc1 = zero_bias, zero_bias, zero_bias
            h2, hn2 = _moe_combine(_tile_dest(dest[:, sl], tm_c), y, gts[:, sl].T,
                                   h1, mod_vec(i, st, 5), g1, sh1, sc1, with_norm=need_norm)
            new.append((h2, hn2))
        h, hn = new[0]
        if upd_ctx:
            hc, hn_c = new[1]
    return h[None]
```

```python
import functools

import jax
import jax.numpy as jnp
from jax import lax
from jax.experimental import pallas as pl
from jax.experimental.pallas import tpu as pltpu

F32 = jnp.float32
BF16 = jnp.bfloat16
U32 = jnp.uint32
I32 = jnp.int32

NORM_EPS = 1e-6
GRID_W = 64
HEAD_DIM = 64
GQA_GROUP = 8
WINDOW = 128
ATTN_BLOCK = 128
ATTN_SCALE = HEAD_DIM ** -0.5
ROPE_BASE = 10000.0
ROPE_PAIRS = HEAD_DIM // 4
NEG_INF = -1e30
LOG2_E = 1.4426950408889634
CHUNK = 128
N_SPATIAL_GROUPS = 8
TOP_K = 4
SWIGLU_LIMIT = 7.0
SWIGLU_ALPHA = 1.702
MOD_CHUNKS = 6
CONV_W = 3
N_MIXERS = 3

LANES = 128
MOE_ROWS = 256
VMEM_LIMIT = 56 * 1024 * 1024


def _params(semantics, vmem=VMEM_LIMIT):
    return pltpu.CompilerParams(dimension_semantics=semantics, vmem_limit_bytes=vmem)


def _store_row_tiles(ref, val):
    rows, d = val.shape
    nch = d // LANES
    for j in range(nch):
        ref[pl.ds(j, rows, stride=nch), :] = val[:, j * LANES:(j + 1) * LANES]


def _gather_pitch(nch):
    tiles = nch // 8
    return 8 * (tiles + 1 - tiles % 2) if nch % 8 == 0 else nch


def _load_row_tiles(ref, base, rows, nch, dtype, pitch):
    return jnp.concatenate(
        [ref[pl.ds(base * pitch + j, rows, stride=pitch), :].astype(dtype) for j in range(nch)], axis=1)


def _rms_scale(x):
    return lax.rsqrt(jnp.mean(x * x, axis=-1, keepdims=True) + NORM_EPS)


def _modulate(x, g, shift, scale):
    return (x * _rms_scale(x) * g) * (1.0 + scale) + shift


def _sigmoid(x):
    return 1.0 / (1.0 + jnp.exp(-x))


def _mod_kernel(cc_ref, w_ref, b_ref, o_ref):
    cc = cc_ref[...]
    cond = cc * _sigmoid(cc)
    o_ref[0] = jnp.dot(cond.astype(BF16), w_ref[0].astype(BF16),
                       preferred_element_type=F32) + b_ref[0]


def _mod_vectors(cc, w_mod, b_mod):
    depth, d, n = w_mod.shape
    tn = 1024
    return pl.pallas_call(
        _mod_kernel,
        out_shape=jax.ShapeDtypeStruct((depth, 8, n), F32),
        grid=(depth, n // tn),
        in_specs=[
            pl.BlockSpec((8, d), lambda l, j: (0, 0)),
            pl.BlockSpec((1, d, tn), lambda l, j: (l, 0, j)),
            pl.BlockSpec((1, 1, tn), lambda l, j: (l, 0, j)),
        ],
        out_specs=pl.BlockSpec((1, 8, tn), lambda l, j: (l, 0, j)),
        compiler_params=_params(("arbitrary", "arbitrary")),
        name="mod_vectors",
    )(cc, w_mod, b_mod.reshape(depth, 1, n))


def _prenorm_kernel(h_ref, g_ref, sh_ref, sc_ref, o_ref):
    o_ref[...] = _modulate(h_ref[...], g_ref[...], sh_ref[...], sc_ref[...]).astype(BF16)


def _prenorm(h, g, shift, scale):
    m, d = h.shape
    tm = 256
    vec = pl.BlockSpec((1, d), lambda i: (0, 0))
    return pl.pallas_call(
        _prenorm_kernel,
        out_shape=jax.ShapeDtypeStruct((m, d), BF16),
        grid=(m // tm,),
        in_specs=[pl.BlockSpec((tm, d), lambda i: (i, 0)), vec, vec, vec],
        out_specs=pl.BlockSpec((tm, d), lambda i: (i, 0)),
        compiler_params=_params(("arbitrary",)),
        name="prenorm",
    )(h, g, shift, scale)


def _mm_kernel(x_ref, w_ref, b_ref, o_ref, *, gelu):
    acc = jnp.dot(x_ref[...], w_ref[...], preferred_element_type=F32) + b_ref[...]
    if gelu:
        inner = 0.7978845608028654 * (acc + 0.044715 * (acc * acc * acc))
        acc = 0.5 * acc * (1.0 + jnp.tanh(inner))
    o_ref[...] = acc.astype(o_ref.dtype)


def _matmul(x, w, b, *, gelu=False):
    m, k = x.shape
    n = w.shape[1]
    tm = 1024 if m % 1024 == 0 else 256
    tn = next(t for t in (1024, 1280, 768, 512, 256) if n % t == 0)
    return pl.pallas_call(
        functools.partial(_mm_kernel, gelu=gelu),
        out_shape=jax.ShapeDtypeStruct((m, n), BF16),
        grid=(n // tn, m // tm),
        in_specs=[
            pl.BlockSpec((tm, k), lambda j, i: (i, 0)),
            pl.BlockSpec((k, tn), lambda j, i: (0, j)),
            pl.BlockSpec((1, tn), lambda j, i: (0, j)),
        ],
        out_specs=pl.BlockSpec((tm, tn), lambda j, i: (i, j)),
        compiler_params=_params(("arbitrary", "arbitrary")),
        name="matmul_gelu" if gelu else "matmul",
    )(x, w, b)


def _conv_kernel(b_ref, c_ref, v_ref, cp_ref, vp_ref, cn_ref, vn_ref, w_ref, o_ref, *, n_tiles):
    i = pl.program_id(0)
    tm = b_ref.shape[0]
    u = c_ref[...].astype(F32) * v_ref[...].astype(F32)
    u_prev = cp_ref[7:8, :].astype(F32) * vp_ref[7:8, :].astype(F32)
    u_next = cn_ref[0:1, :].astype(F32) * vn_ref[0:1, :].astype(F32)
    u_prev = jnp.where(i == 0, 0.0, u_prev)
    u_next = jnp.where(i == n_tiles - 1, 0.0, u_next)
    row = lax.broadcasted_iota(I32, u.shape, 0)
    below = jnp.where(row == 0, u_prev, pltpu.roll(u, 1, axis=0))
    above = jnp.where(row == tm - 1, u_next, pltpu.roll(u, tm - 1, axis=0))
    w = w_ref[...]
    y = w[0:1, :] * below + w[1:2, :] * u + w[2:3, :] * above
    o_ref[...] = (b_ref[...].astype(F32) * y).astype(BF16)


def _short_conv_gate(bcv, conv_w):
    m, n3 = bcv.shape
    d = n3 // 3
    tm = 512 if m % 512 == 0 else 256
    tc = 1024 if d % 1024 == 0 else 512
    nt, nc = m // tm, d // tc
    hb = tm // 8
    last8 = m // 8 - 1
    return pl.pallas_call(
        functools.partial(_conv_kernel, n_tiles=nt),
        out_shape=jax.ShapeDtypeStruct((m, d), BF16),
        grid=(nt, nc),
        in_specs=[
            pl.BlockSpec((tm, tc), lambda i, j: (i, j)),
            pl.BlockSpec((tm, tc), lambda i, j: (i, j + nc)),
            pl.BlockSpec((tm, tc), lambda i, j: (i, j + 2 * nc)),
            pl.BlockSpec((8, tc), lambda i, j: (jnp.maximum(i * hb - 1, 0), j + nc)),
            pl.BlockSpec((8, tc), lambda i, j: (jnp.maximum(i * hb - 1, 0), j + 2 * nc)),
            pl.BlockSpec((8, tc), lambda i, j: (jnp.minimum((i + 1) * hb, last8), j + nc)),
            pl.BlockSpec((8, tc), lambda i, j: (jnp.minimum((i + 1) * hb, last8), j + 2 * nc)),
            pl.BlockSpec((CONV_W, tc), lambda i, j: (0, j)),
        ],
        out_specs=pl.BlockSpec((tm, tc), lambda i, j: (i, j)),
        compiler_params=_params(("arbitrary", "arbitrary")),
        name="short_conv_gate",
    )(bcv, bcv, bcv, bcv, bcv, bcv, bcv, conv_w)


def _head_sumsq(x, seg_ref):
    x2 = x * x
    hi = x2.astype(BF16)
    lo = (x2 - hi.astype(F32)).astype(BF16)
    seg = seg_ref[...]
    return (jnp.dot(hi, seg, preferred_element_type=F32)
            + jnp.dot(lo, seg, preferred_element_type=F32))


def _swap16(x):
    n = x.shape[-1]
    lane = lax.broadcasted_iota(I32, x.shape, x.ndim - 1)
    first = (lane % 32) < 16
    return jnp.where(first, pltpu.roll(x, n - 16, axis=x.ndim - 1), pltpu.roll(x, 16, axis=x.ndim - 1))


def _tile_lanes(x, width):
    return jnp.concatenate([x] * (width // x.shape[-1]), axis=-1)


def _kprep_kernel(k_ref, v_ref, g_ref, cos_ref, sin_ref, seg_ref, kd_ref, vlr_ref, *, rope):
    k = k_ref[...].astype(F32)
    width = k.shape[-1]
    ss = _head_sumsq(k, seg_ref)
    kn = k * lax.rsqrt(ss * (1.0 / HEAD_DIM) + NORM_EPS) * _tile_lanes(g_ref[...], width)
    if rope:
        kn = kn * _tile_lanes(cos_ref[...], width) + _swap16(kn) * _tile_lanes(sin_ref[...], width)
    v = v_ref[...].astype(F32)
    lane = lax.broadcasted_iota(I32, (k.shape[0], LANES), 1)
    left = lane < HEAD_DIM
    for p in range(width // LANES):
        kp = kn[:, p * LANES:(p + 1) * LANES]
        kr = pltpu.roll(kp, HEAD_DIM, axis=1)
        kd_ref[:, (2 * p) * LANES:(2 * p + 1) * LANES] = jnp.where(left, kp, kr).astype(BF16)
        kd_ref[:, (2 * p + 1) * LANES:(2 * p + 2) * LANES] = jnp.where(left, kr, kp).astype(BF16)
        vp = v[:, p * LANES:(p + 1) * LANES]
        vr = pltpu.roll(vp, HEAD_DIM, axis=1)
        base = 4 * p * LANES
        vlr_ref[:, base:base + LANES] = jnp.where(left, vp, 0.0).astype(BF16)
        vlr_ref[:, base + LANES:base + 2 * LANES] = jnp.where(left, 0.0, vr).astype(BF16)
        vlr_ref[:, base + 2 * LANES:base + 3 * LANES] = jnp.where(left, vr, 0.0).astype(BF16)
        vlr_ref[:, base + 3 * LANES:base + 4 * LANES] = jnp.where(left, 0.0, vp).astype(BF16)


def _key_value_prep(qkv, g_k, cos, sin, seg, *, q_width, kv_width, rope):
    m = qkv.shape[0]
    tm = 256
    n_kv = kv_width // HEAD_DIM
    kb = q_width // kv_width
    return pl.pallas_call(
        functools.partial(_kprep_kernel, rope=rope),
        out_shape=(jax.ShapeDtypeStruct((m, n_kv * LANES), BF16),
                   jax.ShapeDtypeStruct((m, n_kv * 2 * LANES), BF16)),
        grid=(m // tm,),
        in_specs=[
            pl.BlockSpec((tm, kv_width), lambda i: (i, kb)),
            pl.BlockSpec((tm, kv_width), lambda i: (i, kb + 1)),
            pl.BlockSpec((1, LANES), lambda i: (0, 0)),
            pl.BlockSpec((tm, LANES), lambda i: (i, 0)),
            pl.BlockSpec((tm, LANES), lambda i: (i, 0)),
            pl.BlockSpec((kv_width, kv_width), lambda i: (0, 0)),
        ],
        out_specs=(pl.BlockSpec((tm, n_kv * LANES), lambda i: (i, 0)),
                   pl.BlockSpec((tm, n_kv * 2 * LANES), lambda i: (i, 0))),
        compiler_params=_params(("arbitrary",)),
        name="key_value_prep_rope" if rope else "key_value_prep",
    )(qkv, qkv, g_k, cos, sin, seg)


def _attn_kernel(sink_ref, q_ref, cos_ref, sin_ref, g_ref, seg_ref,
                 kd0_ref, kd1_ref, kd2_ref, vl0_ref, vl1_ref, vl2_ref, kdc_ref, vlc_ref,
                 o_ref, s_scr, p_scr, *, seq, n_kv):
    n = pl.program_id(0)
    blk = ATTN_BLOCK
    n_band = 3 * blk
    n_keys = s_scr.shape[1]
    gw = GQA_GROUP * HEAD_DIM
    r = lax.broadcasted_iota(I32, (blk, 3 * blk), 0)
    kk = lax.broadcasted_iota(I32, (blk, 3 * blk), 1)
    off = kk - r
    key_pos = n * blk + kk - blk
    mask = ((off >= blk - WINDOW) & (off <= blk + WINDOW) & (key_pos >= 0) & (key_pos < seq))
    lane = lax.broadcasted_iota(I32, (blk, LANES), 1)
    left = lane < HEAD_DIM
    cos = _tile_lanes(cos_ref[...], gw)
    sin = _tile_lanes(sin_ref[...], gw)
    gq = _tile_lanes(g_ref[...], gw)
    nt = (((1,), (1,)), ((), ()))
    for h in range(n_kv):
        q = q_ref[:, h * gw:(h + 1) * gw].astype(F32)
        ss = _head_sumsq(q, seg_ref)
        qn = q * lax.rsqrt(ss * (1.0 / HEAD_DIM) + NORM_EPS) * gq
        qn = ((qn * cos + _swap16(qn) * sin) * (ATTN_SCALE * LOG2_E)).astype(BF16)
        kd = jnp.concatenate([ref[:, h * LANES:(h + 1) * LANES] for ref in (kd0_ref, kd1_ref, kd2_ref)], axis=0)
        kdc = kdc_ref[:, h * LANES:(h + 1) * LANES]
        vb = 2 * h * LANES
        vl = jnp.concatenate([ref[:, vb:vb + LANES] for ref in (vl0_ref, vl1_ref, vl2_ref)], axis=0)
        vr = jnp.concatenate([ref[:, vb + LANES:vb + 2 * LANES] for ref in (vl0_ref, vl1_ref, vl2_ref)], axis=0)
        vlc = vlc_ref[:, vb:vb + LANES]
        vrc = vlc_ref[:, vb + LANES:vb + 2 * LANES]

        for g in range(GQA_GROUP):
            pair = qn[:, (g // 2) * LANES:(g // 2 + 1) * LANES]
            qg = jnp.where(left, pair, 0.0) if g % 2 == 0 else jnp.where(left, 0.0, pair)
            s = lax.dot_general(qg, kd, nt, preferred_element_type=F32)
            s_scr[g * blk:(g + 1) * blk, 0:n_band] = jnp.where(mask, s, NEG_INF)
            s_scr[g * blk:(g + 1) * blk, n_band:n_keys] = lax.dot_general(qg, kdc, nt, preferred_element_type=F32)

        sk = jnp.concatenate([jnp.full((blk, 1), sink_ref[h * GQA_GROUP + g] * LOG2_E, F32)
                              for g in range(GQA_GROUP)], axis=0)
        mx = jnp.maximum(jnp.max(s_scr[...], axis=-1, keepdims=True), sk)
        e = jnp.exp2(s_scr[...] - mx)
        s_scr[...] = e
        inv = 1.0 / (jnp.sum(e, axis=-1, keepdims=True) + jnp.exp2(sk - mx))
        p_scr[...] = (s_scr[...] * inv).astype(BF16)

        for p in range(GQA_GROUP // 2):
            r0, r1, r2 = 2 * p * blk, (2 * p + 1) * blk, (2 * p + 2) * blk
            acc = (jnp.dot(p_scr[r0:r1, 0:n_band], vl, preferred_element_type=F32)
                   + jnp.dot(p_scr[r0:r1, n_band:n_keys], vlc, preferred_element_type=F32)
                   + jnp.dot(p_scr[r1:r2, 0:n_band], vr, preferred_element_type=F32)
                   + jnp.dot(p_scr[r1:r2, n_band:n_keys], vrc, preferred_element_type=F32))
            o_ref[:, h * gw + p * LANES:h * gw + (p + 1) * LANES] = acc.astype(BF16)


def _window_attention(qkv, kd, vlr, kdc, vlrc, sink, g_q, cos, sin, seg, *, q_width):
    seq = qkv.shape[0]
    n_kv = kd.shape[1] // LANES
    nb = seq // ATTN_BLOCK
    n_ctx = kdc.shape[0]
    gw = GQA_GROUP * HEAD_DIM

    def band(shift, width):
        return pl.BlockSpec((ATTN_BLOCK, width),
                            lambda n, s: (jnp.clip(n + shift, 0, nb - 1), 0))

    grid_spec = pltpu.PrefetchScalarGridSpec(
        num_scalar_prefetch=1,
        grid=(nb,),
        in_specs=[
            pl.BlockSpec((ATTN_BLOCK, q_width), lambda n, s: (n, 0)),
            pl.BlockSpec((ATTN_BLOCK, LANES), lambda n, s: (n, 0)),
            pl.BlockSpec((ATTN_BLOCK, LANES), lambda n, s: (n, 0)),
            pl.BlockSpec((1, LANES), lambda n, s: (0, 0)),
            pl.BlockSpec((gw, gw), lambda n, s: (0, 0)),
            band(-1, kd.shape[1]), band(0, kd.shape[1]), band(1, kd.shape[1]),
            band(-1, vlr.shape[1]), band(0, vlr.shape[1]), band(1, vlr.shape[1]),
            pl.BlockSpec((n_ctx, kdc.shape[1]), lambda n, s: (0, 0)),
            pl.BlockSpec((n_ctx, vlrc.shape[1]), lambda n, s: (0, 0)),
        ],
        out_specs=pl.BlockSpec((ATTN_BLOCK, q_width), lambda n, s: (n, 0)),
        scratch_shapes=[pltpu.VMEM((GQA_GROUP * ATTN_BLOCK, 3 * ATTN_BLOCK + n_ctx), F32),
                        pltpu.VMEM((GQA_GROUP * ATTN_BLOCK, 3 * ATTN_BLOCK + n_ctx), BF16)],
    )
    return pl.pallas_call(
        functools.partial(_attn_kernel, seq=seq, n_kv=n_kv),
        out_shape=jax.ShapeDtypeStruct((seq, q_width), BF16),
        grid_spec=grid_spec,
        compiler_params=_params(("arbitrary",)),
        name="window_attention",
    )(sink, qkv, cos, sin, g_q, seg, kd, kd, kd, vlr, vlr, vlr, kdc, vlrc)


def _sgu_kernel(u_ref, v_ref, g_ref, ws_ref, bs_ref, o_ref):
    v = v_ref[...].astype(F32)
    vn = (v * _rms_scale(v) * g_ref[...]).astype(BF16)
    d = v.shape[-1]
    gwid = d // N_SPATIAL_GROUPS
    for g in range(N_SPATIAL_GROUPS):
        cols = slice(g * gwid, (g + 1) * gwid)
        vs = jnp.dot(ws_ref[g].astype(BF16), vn[:, cols], preferred_element_type=F32) + bs_ref[:, g:g + 1]
        o_ref[:, cols] = (u_ref[:, cols].astype(F32) * vs).astype(BF16)


def _spatial_gate(uv, g_v, w_s, b_s_t):
    m, d2 = uv.shape
    d = d2 // 2
    return pl.pallas_call(
        _sgu_kernel,
        out_shape=jax.ShapeDtypeStruct((m, d), BF16),
        grid=(m // CHUNK,),
        in_specs=[
            pl.BlockSpec((CHUNK, d), lambda i: (i, 0)),
            pl.BlockSpec((CHUNK, d), lambda i: (i, 1)),
            pl.BlockSpec((1, d), lambda i: (0, 0)),
            pl.BlockSpec((N_SPATIAL_GROUPS, CHUNK, CHUNK), lambda i: (0, 0, 0)),
            pl.BlockSpec((CHUNK, N_SPATIAL_GROUPS), lambda i: (0, 0)),
        ],
        out_specs=pl.BlockSpec((CHUNK, d), lambda i: (i, 0)),
        compiler_params=_params(("arbitrary",)),
        name="spatial_gate",
    )(uv, uv, g_v, w_s, b_s_t)


def _out_router_kernel(z_ref, w_ref, b_ref, h_ref, gate_ref, g_ref, sh_ref, sc_ref,
                       rw_ref, rb_ref, cnt0_ref,
                       h1_ref, xw_ref, idx_ref, gts_ref, rank_ref, cnt_ref, run_ref):
    i = pl.program_id(0)
    tm, d = h_ref.shape
    n_exp = rw_ref.shape[0]

    @pl.when(i == 0)
    def _():
        run_ref[...] = cnt0_ref[...]

    out = jnp.dot(z_ref[...], w_ref[...], preferred_element_type=F32) + b_ref[...]
    h1 = h_ref[...] + gate_ref[...] * out
    h1_ref[...] = h1
    y = _modulate(h1, g_ref[...], sh_ref[...], sc_ref[...])
    yb = y.astype(BF16)
    _store_row_tiles(xw_ref, y)

    logits = lax.dot_general(rw_ref[...], yb, (((1,), (1,)), ((), ())),
                             preferred_element_type=F32) + rb_ref[:, 0:1]
    e_iota = lax.broadcasted_iota(I32, (n_exp, tm), 0).astype(F32)
    vals, idxs = [], []
    cur = logits
    for _k in range(TOP_K):
        mx = jnp.max(cur, axis=0, keepdims=True)
        ix = jnp.min(jnp.where(cur == mx, e_iota, float(n_exp)), axis=0, keepdims=True)
        vals.append(mx)
        idxs.append(ix)
        cur = jnp.where(e_iota == ix, -jnp.inf, cur)
    exps = [jnp.exp(v - vals[0]) for v in vals]
    den = exps[0] + exps[1] + exps[2] + exps[3]
    sel = jnp.zeros((n_exp, tm), F32)
    for ix in idxs:
        sel = sel + jnp.where(e_iota == ix, 1.0, 0.0)
    before = (lax.broadcasted_iota(I32, (tm, tm), 0) < lax.broadcasted_iota(I32, (tm, tm), 1))
    prior = jnp.dot(sel.astype(BF16), jnp.where(before, 1.0, 0.0).astype(BF16),
                    preferred_element_type=F32) + run_ref[:, 0:1]
    zeros = jnp.zeros((8 - TOP_K, tm), F32)
    ranks = [jnp.sum(jnp.where(e_iota == ix, prior, 0.0), axis=0, keepdims=True) for ix in idxs]
    idx_ref[...] = jnp.concatenate(idxs + [zeros], axis=0).astype(I32)
    gts_ref[...] = jnp.concatenate([e / den for e in exps] + [zeros], axis=0)
    rank_ref[...] = jnp.concatenate(ranks + [zeros], axis=0).astype(I32)
    run_ref[...] = run_ref[...] + jnp.sum(sel, axis=1, keepdims=True)
    cnt_ref[...] = run_ref[...]


def _out_proj_router(z, w, b, h, gate, g2, sh2, sc2, r_w_t, r_b, cnt0):
    m, d = h.shape
    k = z.shape[1]
    n_exp = r_w_t.shape[0]
    tm = 256
    nch = d // LANES
    vec = pl.BlockSpec((1, d), lambda i: (0, 0))
    row = lambda width: pl.BlockSpec((tm, width), lambda i: (i, 0))
    tok = pl.BlockSpec((8, tm), lambda i: (0, i))
    cnt = pl.BlockSpec((n_exp, LANES), lambda i: (0, 0))
    return pl.pallas_call(
        _out_router_kernel,
        out_shape=(jax.ShapeDtypeStruct((m, d), F32),
                   jax.ShapeDtypeStruct((m * nch, LANES), F32),
                   jax.ShapeDtypeStruct((8, m), I32),
                   jax.ShapeDtypeStruct((8, m), F32),
                   jax.ShapeDtypeStruct((8, m), I32),
                   jax.ShapeDtypeStruct((n_exp, LANES), F32)),
        grid=(m // tm,),
        in_specs=[row(k), pl.BlockSpec((k, d), lambda i: (0, 0)), vec, row(d), vec, vec, vec, vec,
                  pl.BlockSpec((n_exp, d), lambda i: (0, 0)), cnt, cnt],
        out_specs=(row(d), pl.BlockSpec((tm * nch, LANES), lambda i: (i, 0)), tok, tok, tok, cnt),
        scratch_shapes=[pltpu.VMEM((n_exp, LANES), F32)],
        compiler_params=_params(("arbitrary",)),
        name="out_proj_router",
    )(z, w, b, h, gate, g2, sh2, sc2, r_w_t, r_b, cnt0)


GATHER_UNROLL = 16
WEIGHT_CHUNKS = 4


def _gather_tokens(idx_ref, src_hbm, dst_ref, sem, n_tok, nch, *, unroll, priorities=(0, 1)):
    pitch = _gather_pitch(nch)

    def issue(r, u):
        t = idx_ref[0, 0, r]
        src = src_hbm.at[pl.ds(pl.multiple_of(t * nch, 8), nch)]
        dst = dst_ref.at[pl.ds(pl.multiple_of(r * pitch, 8), nch)]
        pltpu.make_async_copy(src, dst, sem).start(priority=priorities[u % len(priorities)])

    if unroll >= n_tok:
        for r in range(n_tok):
            issue(r, r)
        return

    def body(c, carry):
        for u in range(unroll):
            issue(c * unroll + u, u)
        return carry
    lax.fori_loop(0, n_tok // unroll, body, 0)


def _wait_tokens(src_hbm, dst_ref, sem, n_tok, nch):
    n_rows = n_tok * nch
    pltpu.make_async_copy(src_hbm.at[pl.ds(0, n_rows)], dst_ref.at[pl.ds(0, n_rows)], sem).wait()


def _moe_ffn_kernel(be_ref, nu_ref, nx_ref, first_ref, next_ref, x_hbm, w1_hbm, w2_hbm, b1_ref, b2_ref,
                    o_ref, xbuf, gsem, w1s, w2s, wsem, w1b, w2p, pair, *, layer, nch):
    b = pl.program_id(0)
    n_used = nu_ref[0]
    rows = MOE_ROWS
    slot = b % 2

    def chunk_copies(w_hbm, stage, sem, e):
        n = stage.shape[0] // WEIGHT_CHUNKS
        return [pltpu.make_async_copy(w_hbm.at[layer, e, pl.ds(i * n, n)], stage.at[pl.ds(i * n, n)], sem)
                for i in range(WEIGHT_CHUNKS)]

    def start_all(copies):
        for i, cp in enumerate(copies):
            cp.start(priority=i % 2)

    def wait_all(copies):
        for cp in copies:
            cp.wait()

    @pl.when(b == 0)
    def _():
        start_all(chunk_copies(w1_hbm, w1s, wsem.at[0], be_ref[0]))
        start_all(chunk_copies(w2_hbm, w2s, wsem.at[1], be_ref[0]))
        _gather_tokens(first_ref, x_hbm, xbuf.at[0], gsem.at[0], rows, nch, unroll=GATHER_UNROLL)

    @pl.when(b >= n_used)
    def _():
        o_ref[...] = jnp.zeros(o_ref.shape, o_ref.dtype)

    @pl.when(b < n_used)
    def _():
        e = be_ref[b]
        changed = jnp.logical_or(b == 0, e != be_ref[jnp.maximum(b - 1, 0)])

        @pl.when(changed)
        def _():
            nxt = nx_ref[b]
            wait_all(chunk_copies(w1_hbm, w1s, wsem.at[0], e))
            w1b[...] = w1s[...].astype(BF16)

            @pl.when(nxt >= 0)
            def _():
                start_all(chunk_copies(w1_hbm, w1s, wsem.at[0], nxt))

            wait_all(chunk_copies(w2_hbm, w2s, wsem.at[1], e))
            half = LANES // 2
            for m in range(w2p.shape[0] // LANES):
                for c in range(nch):
                    cols = slice(c * LANES, (c + 1) * LANES)
                    pair[pl.ds(c * LANES, half, stride=2), :] = w2s[m * LANES:m * LANES + half, cols]
                    pair[pl.ds(c * LANES + 1, half, stride=2), :] = w2s[m * LANES + half:(m + 1) * LANES, cols]
                    w2p[m * LANES:(m + 1) * LANES, cols] = pair[c * LANES:(c + 1) * LANES, :].astype(BF16)

            @pl.when(nxt >= 0)
            def _():
                start_all(chunk_copies(w2_hbm, w2s, wsem.at[1], nxt))

        _gather_tokens(next_ref, x_hbm, xbuf.at[1 - slot], gsem.at[1 - slot], rows, nch, unroll=rows)
        _wait_tokens(x_hbm, xbuf.at[slot], gsem.at[slot], rows, nch)
        x = _load_row_tiles(xbuf.at[slot], 0, rows, nch, BF16, _gather_pitch(nch))
        hcat = jnp.dot(x, w1b[...], preferred_element_type=F32) + b1_ref[0, 0]
        glu = jnp.minimum(hcat, SWIGLU_LIMIT)
        gact = glu * _sigmoid(SWIGLU_ALPHA * glu)
        lin = jnp.clip(hcat, -SWIGLU_LIMIT, SWIGLU_LIMIT) + 1.0
        lane = lax.broadcasted_iota(I32, (rows, LANES), 1)
        even = (lane % 2) == 0
        acts = []
        for m in range(hcat.shape[1] // (2 * LANES)):
            a0, a1, a2 = 2 * m * LANES, (2 * m + 1) * LANES, (2 * m + 2) * LANES
            pa = gact[:, a0:a1] * pltpu.roll(lin[:, a0:a1], LANES - 1, axis=1)
            pb = pltpu.roll(gact[:, a1:a2], 1, axis=1) * lin[:, a1:a2]
            acts.append(jnp.where(even, pa, pb).astype(BF16))
        act = jnp.concatenate(acts, axis=1)
        y = jnp.dot(act, w2p[...], preferred_element_type=F32) + b2_ref[0, 0]
        _store_row_tiles(o_ref, y)

        @pl.when(b == n_used - 1)
        def _():
            _wait_tokens(x_hbm, xbuf.at[1 - slot], gsem.at[1 - slot], rows, nch)


def _moe_ffn(block_e, n_used, next_e, src_tok, xrows, w1, b1, w2, b2, layer):
    nb = src_tok.shape[0]
    rows = src_tok.shape[2]
    depth, n_exp, d, f2 = w1.shape
    f = f2 // 2
    nch = d // LANES
    last_used = lambda b, be, nu, nx: jnp.minimum(b + 1, nu[0] - 1)
    grid_spec = pltpu.PrefetchScalarGridSpec(
        num_scalar_prefetch=3,
        grid=(nb,),
        in_specs=[
            pl.BlockSpec((1, 1, rows), lambda b, be, nu, nx: (0, 0, 0), memory_space=pltpu.SMEM),
            pl.BlockSpec((1, 1, rows), lambda b, be, nu, nx: (last_used(b, be, nu, nx), 0, 0),
                         memory_space=pltpu.SMEM),
            pl.BlockSpec(memory_space=pl.ANY),
            pl.BlockSpec(memory_space=pl.ANY),
            pl.BlockSpec(memory_space=pl.ANY),
            pl.BlockSpec((1, 1, 1, f2), lambda b, be, nu, nx: (layer, be[b], 0, 0)),
            pl.BlockSpec((1, 1, 1, d), lambda b, be, nu, nx: (layer, be[b], 0, 0)),
        ],
        out_specs=pl.BlockSpec((rows * nch, LANES), lambda b, be, nu, nx: (b, 0)),
        scratch_shapes=[pltpu.VMEM((2, rows * _gather_pitch(nch), LANES), F32),
                        pltpu.SemaphoreType.DMA((2,)),
                        pltpu.VMEM((d, f2), F32),
                        pltpu.VMEM((f, d), F32),
                        pltpu.SemaphoreType.DMA((2,)),
                        pltpu.VMEM((d, f2), BF16),
                        pltpu.VMEM((f, d), BF16),
                        pltpu.VMEM((nch * LANES, LANES), F32)],
    )
    return pl.pallas_call(
        functools.partial(_moe_ffn_kernel, layer=layer, nch=nch),
        out_shape=jax.ShapeDtypeStruct((nb * rows * nch, LANES), F32),
        grid_spec=grid_spec,
        compiler_params=_params(("arbitrary",)),
        name="moe_ffn",
    )(block_e, n_used, next_e, src_tok, src_tok, xrows, w1, w2,
      b1.reshape(depth, n_exp, 1, f2), b2.reshape(depth, n_exp, 1, d))


def _combine_kernel(first_ref, next_ref, y_hbm, gts_ref, h_ref, gate_ref, g_ref, sh_ref, sc_ref,
                    h2_ref, *rest, n_tiles, with_norm):
    if with_norm:
        hn_ref, ybuf, sem = rest
    else:
        ybuf, sem = rest
    i = pl.program_id(0)
    tm, d = h_ref.shape
    nch = d // LANES
    rows = TOP_K * tm
    slot = i % 2

    @pl.when(i == 0)
    def _():
        _gather_tokens(first_ref, y_hbm, ybuf.at[0], sem.at[0], rows, nch, unroll=GATHER_UNROLL)

    @pl.when(i + 1 < n_tiles)
    def _():
        _gather_tokens(next_ref, y_hbm, ybuf.at[1 - slot], sem.at[1 - slot], rows, nch, unroll=GATHER_UNROLL)

    _wait_tokens(y_hbm, ybuf.at[slot], sem.at[slot], rows, nch)
    pitch = _gather_pitch(nch)
    gks = [gts_ref[:, k:k + 1] for k in range(TOP_K)]
    pieces = []
    for j in range(nch):
        acc = None
        for k in range(TOP_K):
            part = ybuf[slot, pl.ds(k * tm * pitch + j, tm, stride=pitch), :] * gks[k]
            acc = part if acc is None else acc + part
        pieces.append(acc)
    h2 = h_ref[...] + gate_ref[...] * jnp.concatenate(pieces, axis=1)
    h2_ref[...] = h2
    if with_norm:
        hn_ref[...] = _modulate(h2, g_ref[...], sh_ref[...], sc_ref[...]).astype(BF16)


def _moe_combine(dest, y, gts_t, h, gate, g1, sh1, sc1, *, with_norm):
    m, d = h.shape
    n_tiles = dest.shape[0]
    tm = m // n_tiles
    vec = pl.BlockSpec((1, d), lambda i: (0, 0))
    row = pl.BlockSpec((tm, d), lambda i: (i, 0))
    out_shape = [jax.ShapeDtypeStruct((m, d), F32)]
    out_specs = [row]
    if with_norm:
        out_shape.append(jax.ShapeDtypeStruct((m, d), BF16))
        out_specs.append(row)
    res = pl.pallas_call(
        functools.partial(_combine_kernel, n_tiles=n_tiles, with_norm=with_norm),
        out_shape=tuple(out_shape),
        grid=(n_tiles,),
        in_specs=[
            pl.BlockSpec((1, 1, TOP_K * tm), lambda i: (0, 0, 0), memory_space=pltpu.SMEM),
            pl.BlockSpec((1, 1, TOP_K * tm), lambda i: (jnp.minimum(i + 1, n_tiles - 1), 0, 0),
                         memory_space=pltpu.SMEM),
            pl.BlockSpec(memory_space=pl.ANY),
            pl.BlockSpec((tm, 8), lambda i: (i, 0)),
            row, vec, vec, vec, vec,
        ],
        out_specs=tuple(out_specs),
        scratch_shapes=[pltpu.VMEM((2, TOP_K * tm * _gather_pitch(d // LANES), LANES), F32),
                        pltpu.SemaphoreType.DMA((2,))],
        compiler_params=_params(("arbitrary",)),
        name="moe_combine_norm" if with_norm else "moe_combine",
    )(dest, dest, y, gts_t, h, gate, g1, sh1, sc1)
    return res if with_norm else (res[0], None)


def _dispatch_plan(idx, rank, counts, rows):
    n_exp = counts.shape[0]
    n_tok = idx.shape[1]
    n_assign = TOP_K * n_tok
    nb = n_assign // rows + n_exp
    blocks_e = (counts + rows - 1) // rows
    blk_end = jnp.cumsum(blocks_e)
    blk_start = blk_end - blocks_e
    n_used = blk_end[-1]
    e_ids = jnp.arange(n_exp, dtype=I32)
    chosen = idx[None, :, :] == e_ids[:, None, None]
    dest = jnp.sum(jnp.where(chosen, blk_start[:, None, None], 0), axis=0) * rows + rank
    blk = jnp.minimum(jnp.arange(nb, dtype=I32), n_used - 1)
    block_e = jnp.minimum(jnp.sum((blk[:, None] >= blk_end[None, :]).astype(I32), axis=1), n_exp - 1)
    group_end = jnp.sum(jnp.where(block_e[:, None] == e_ids[None, :], blk_end[None, :], 0), axis=1)
    follower = jnp.sum(jnp.where(group_end[:, None] == jnp.arange(nb, dtype=I32)[None, :],
                                 block_e[None, :], 0), axis=1)
    next_e = jnp.where(group_end < n_used, follower, -1).astype(I32)
    tok = jnp.broadcast_to(jnp.arange(n_tok, dtype=I32)[None, :], dest.shape)
    src_tok = jnp.zeros((nb * rows,), I32).at[dest.reshape(-1)].set(
        tok.reshape(-1), unique_indices=True, mode='promise_in_bounds')
    return block_e.astype(I32), n_used.reshape(1).astype(I32), next_e, src_tok.reshape(nb, 1, rows), dest


def _tile_dest(dest, tm):
    n_tok = dest.shape[1]
    return dest.reshape(TOP_K, n_tok // tm, tm).transpose(1, 0, 2).reshape(n_tok // tm, 1, TOP_K * tm)


def _rope_tables(seq):
    n_rows = seq // GRID_W
    inv_freq = ROPE_BASE ** (-jnp.arange(ROPE_PAIRS, dtype=F32) / ROPE_PAIRS)
    ang_r = jnp.arange(n_rows, dtype=F32)[:, None] * inv_freq[None, :]
    ang_c = jnp.arange(GRID_W, dtype=F32)[:, None] * inv_freq[None, :]

    def per_token(tab_r, tab_c):
        tr = jnp.broadcast_to(tab_r[:, None, :], (n_rows, GRID_W, tab_r.shape[1])).reshape(seq, -1)
        tc = jnp.broadcast_to(tab_c[None, :, :], (n_rows, GRID_W, tab_c.shape[1])).reshape(seq, -1)
        return jnp.concatenate([tr, tc, tr, tc], axis=1)

    cos_r, sin_r, cos_c, sin_c = jnp.cos(ang_r), jnp.sin(ang_r), jnp.cos(ang_c), jnp.sin(ang_c)
    cos = per_token(jnp.concatenate([cos_r, cos_r], axis=1), jnp.concatenate([cos_c, cos_c], axis=1))
    sin = per_token(jnp.concatenate([-sin_r, sin_r], axis=1), jnp.concatenate([-sin_c, sin_c], axis=1))
    return cos, sin


def _head_segments(width):
    seg = jnp.arange(width, dtype=I32) // HEAD_DIM
    return (seg[:, None] == seg[None, :]).astype(BF16)


def _layer_plan(depth):
    counts = [0] * N_MIXERS
    plan = []
    for i in range(depth):
        kind = i % N_MIXERS
        plan.append((kind, counts[kind]))
        counts[kind] += 1
    return plan


def kernel(x, c, ctx, c_ctx, w_mod, b_mod, g_norm1, g_norm2, a_w_in, a_conv, a_w_out, b_w_qkv, b_b_qkv, b_g_q, b_g_k, b_sink, b_w_o, b_b_o, c_w_in, c_b_in, c_g_v, c_w_s, c_b_s, c_w_out, c_b_out, r_w, r_b, e_w1, e_b1, e_w2, e_b2):
    bsz, seq, d = x.shape
    assert bsz == 1
    depth = w_mod.shape[0]
    n_exp = r_w.shape[2]
    n_ctx = ctx.shape[1]
    plan = _layer_plan(depth)
    readers = [i for i, (kind, _) in enumerate(plan) if kind == 1]
    last_reader = readers[-1] if readers else -1
    q_width = d
    kv_width = (b_w_qkv.shape[2] - q_width) // 2
    rows = MOE_ROWS
    tm_c = 128

    cc = jnp.concatenate([c, c_ctx[None, :], jnp.zeros((6, d), F32)], axis=0)
    mod = _mod_vectors(cc, w_mod, b_mod)

    def mod_vec(layer, stream, chunk):
        return mod[layer, stream:stream + 1, chunk * d:(chunk + 1) * d]

    def vec(a):
        return a.reshape(1, -1)

    cos, sin = _rope_tables(seq)
    seg_q = _head_segments(GQA_GROUP * HEAD_DIM)
    seg_k = _head_segments(kv_width)
    zero_bias = jnp.zeros((1, d), F32)
    zero_cnt = jnp.zeros((n_exp, LANES), F32)

    h = x[0]
    hc = ctx[0]
    hn = _prenorm(h, vec(g_norm1[0]), mod_vec(0, 0, 0), mod_vec(0, 0, 1))
    hn_c = None
    if 0 <= last_reader:
        hn_c = _prenorm(hc, vec(g_norm1[0]), mod_vec(0, 1, 0), mod_vec(0, 1, 1))

    for i, (kind, s) in enumerate(plan):
        upd_ctx = i < last_reader
        read_ctx = i <= last_reader
        streams = [(0, hn, h)] + ([(1, hn_c, hc)] if upd_ctx else [])

        zs = []
        if kind == 0:
            w_in = a_w_in[s].astype(BF16)
            w_out, b_out = a_w_out[s].astype(BF16), zero_bias
            for _, hn_s, _h in streams:
                bcv = _matmul(hn_s, w_in, jnp.zeros((1, 3 * d), F32))
                zs.append(_short_conv_gate(bcv, a_conv[s]))
        elif kind == 1:
            w_qkv = b_w_qkv[s].astype(BF16)
            w_out, b_out = b_w_o[s].astype(BF16), vec(b_b_o[s])
            g_q = vec(jnp.concatenate([b_g_q[s], b_g_q[s]]))
            g_k = vec(jnp.concatenate([b_g_k[s], b_g_k[s]]))
            qkv = _matmul(hn, w_qkv, vec(b_b_qkv[s]))
            qkv_c = _matmul(hn_c, w_qkv, vec(b_b_qkv[s]))
            kd, vlr = _key_value_prep(qkv, g_k, cos, sin, seg_k, q_width=q_width, kv_width=kv_width, rope=True)
            kdc, vlrc = _key_value_prep(qkv_c, g_k, cos[:n_ctx], sin[:n_ctx], seg_k,
                                        q_width=q_width, kv_width=kv_width, rope=False)
            zs.append(_window_attention(qkv, kd, vlr, kdc, vlrc, b_sink[s], g_q, cos, sin, seg_q,
                                        q_width=q_width))
            assert not upd_ctx
        else:
            w_in = c_w_in[s].astype(BF16)
            w_out, b_out = c_w_out[s].astype(BF16), vec(c_b_out[s])
            for _, hn_s, _h in streams:
                uv = _matmul(hn_s, w_in, vec(c_b_in[s]), gelu=True)
                zs.append(_spatial_gate(uv, vec(c_g_v[s]), c_w_s[s], c_b_s[s].T))

        r_w_t = r_w[i].T.astype(BF16)
        r_b_col = jnp.broadcast_to(r_b[i][:, None], (n_exp, LANES))
        outs = []
        cnt = zero_cnt
        for (st, _hn, h_s), z in zip(streams, zs):
            res = _out_proj_router(z, w_out, b_out, h_s, mod_vec(i, st, 2), vec(g_norm2[i]),
                                   mod_vec(i, st, 3), mod_vec(i, st, 4), r_w_t, r_b_col, cnt)
            outs.append(res)
            cnt = res[5]
        if len(outs) == 1:
            h1s, xw, idx, gts, rank = [outs[0][0]], outs[0][1], outs[0][2], outs[0][3], outs[0][4]
        else:
            h1s = [o[0] for o in outs]
            xw, idx, gts, rank = [jnp.concatenate([o[j] for o in outs], axis=(0 if j == 1 else 1))
                                  for j in (1, 2, 3, 4)]
        counts = cnt[:, 0].astype(I32)

        block_e, n_used, next_e, src_tok, dest = _dispatch_plan(idx[:TOP_K], rank[:TOP_K], counts, rows)
        y = _moe_ffn(block_e, n_used, next_e, src_tok, xw, e_w1, e_b1, e_w2, e_b2, i)

        last = i == depth - 1
        new = []
        off = 0
        for (st, _hn, _h), h1 in zip(streams, h1s):
            n_tok = h1.shape[0]
            sl = slice(off, off + n_tok)
            off += n_tok
            nxt = i + 1
            need_norm = (not last) and (st == 0 or nxt <= last_reader)
            if need_norm:
                g1, sh1, sc1 = vec(g_norm1[nxt]), mod_vec(nxt, st, 0), mod_vec(nxt, st, 1)
            else:
                g1, sh1, sc1 = zero_bias, zero_bias, zero_bias
            h2, hn2 = _moe_combine(_tile_dest(dest[:, sl], tm_c), y, gts[:, sl].T,
                                   h1, mod_vec(i, st, 5), g1, sh1, sc1, with_norm=need_norm)
            new.append((h2, hn2))
        h, hn = new[0]
        if upd_ctx:
            hc, hn_c = new[1]
    return h[None]
```

```python
import functools

import jax
import jax.numpy as jnp
from jax import lax
from jax.experimental import pallas as pl
from jax.experimental.pallas import tpu as pltpu

F32 = jnp.float32
BF16 = jnp.bfloat16
U32 = jnp.uint32
I32 = jnp.int32

NORM_EPS = 1e-6
GRID_W = 64
HEAD_DIM = 64
GQA_GROUP = 8
WINDOW = 128
ATTN_BLOCK = 128
ATTN_SCALE = HEAD_DIM ** -0.5
ROPE_BASE = 10000.0
ROPE_PAIRS = HEAD_DIM // 4
NEG_INF = -1e30
LOG2_E = 1.4426950408889634
CHUNK = 128
N_SPATIAL_GROUPS = 8
TOP_K = 4
SWIGLU_LIMIT = 7.0
SWIGLU_ALPHA = 1.702
MOD_CHUNKS = 6
CONV_W = 3
N_MIXERS = 3

LANES = 128
MOE_ROWS = 256
VMEM_LIMIT = 56 * 1024 * 1024


def _params(semantics, vmem=VMEM_LIMIT):
    return pltpu.CompilerParams(dimension_semantics=semantics, vmem_limit_bytes=vmem)


def _store_row_tiles(ref, val):
    rows, d = val.shape
    nch = d // LANES
    for j in range(nch):
        ref[pl.ds(j, rows, stride=nch), :] = val[:, j * LANES:(j + 1) * LANES]


def _gather_pitch(nch):
    tiles = nch // 8
    return 8 * (tiles + 1 - tiles % 2) if nch % 8 == 0 else nch


def _load_row_tiles(ref, base, rows, nch, dtype, pitch):
    return jnp.concatenate(
        [ref[pl.ds(base * pitch + j, rows, stride=pitch), :].astype(dtype) for j in range(nch)], axis=1)


def _rms_scale(x):
    return lax.rsqrt(jnp.mean(x * x, axis=-1, keepdims=True) + NORM_EPS)


def _modulate(x, g, shift, scale):
    return (x * _rms_scale(x) * g) * (1.0 + scale) + shift


def _sigmoid(x):
    return 1.0 / (1.0 + jnp.exp(-x))


def _mod_kernel(cc_ref, w_ref, b_ref, o_ref):
    cc = cc_ref[...]
    cond = cc * _sigmoid(cc)
    o_ref[0] = jnp.dot(cond.astype(BF16), w_ref[0].astype(BF16),
                       preferred_element_type=F32) + b_ref[0]


def _mod_vectors(cc, w_mod, b_mod):
    depth, d, n = w_mod.shape
    tn = 1024
    return pl.pallas_call(
        _mod_kernel,
        out_shape=jax.ShapeDtypeStruct((depth, 8, n), F32),
        grid=(depth, n // tn),
        in_specs=[
            pl.BlockSpec((8, d), lambda l, j: (0, 0)),
            pl.BlockSpec((1, d, tn), lambda l, j: (l, 0, j)),
            pl.BlockSpec((1, 1, tn), lambda l, j: (l, 0, j)),
        ],
        out_specs=pl.BlockSpec((1, 8, tn), lambda l, j: (l, 0, j)),
        compiler_params=_params(("arbitrary", "arbitrary")),
        name="mod_vectors",
    )(cc, w_mod, b_mod.reshape(depth, 1, n))


def _prenorm_kernel(h_ref, g_ref, sh_ref, sc_ref, o_ref):
    o_ref[...] = _modulate(h_ref[...], g_ref[...], sh_ref[...], sc_ref[...]).astype(BF16)


def _prenorm(h, g, shift, scale):
    m, d = h.shape
    tm = 256
    vec = pl.BlockSpec((1, d), lambda i: (0, 0))
    return pl.pallas_call(
        _prenorm_kernel,
        out_shape=jax.ShapeDtypeStruct((m, d), BF16),
        grid=(m // tm,),
        in_specs=[pl.BlockSpec((tm, d), lambda i: (i, 0)), vec, vec, vec],
        out_specs=pl.BlockSpec((tm, d), lambda i: (i, 0)),
        compiler_params=_params(("arbitrary",)),
        name="prenorm",
    )(h, g, shift, scale)


def _mm_kernel(x_ref, w_ref, b_ref, o_ref, wb_ref, *, gelu):
    @pl.when(pl.program_id(1) == 0)
    def _():
        wb_ref[...] = w_ref[0].astype(BF16)

    acc = jnp.dot(x_ref[...], wb_ref[...], preferred_element_type=F32) + b_ref[...]
    if gelu:
        inner = 0.7978845608028654 * (acc + 0.044715 * (acc * acc * acc))
        acc = 0.5 * acc * (1.0 + jnp.tanh(inner))
    o_ref[...] = acc.astype(o_ref.dtype)


def _matmul(x, w, b, *, layer, gelu=False):
    m, k = x.shape
    n = w.shape[2]
    tm = 1024 if m % 1024 == 0 else 256
    tn = next(t for t in (1024, 1280, 768, 512, 256) if n % t == 0)
    return pl.pallas_call(
        functools.partial(_mm_kernel, gelu=gelu),
        out_shape=jax.ShapeDtypeStruct((m, n), BF16),
        grid=(n // tn, m // tm),
        in_specs=[
            pl.BlockSpec((tm, k), lambda j, i: (i, 0)),
            pl.BlockSpec((1, k, tn), lambda j, i: (layer, 0, j)),
            pl.BlockSpec((1, tn), lambda j, i: (0, j)),
        ],
        out_specs=pl.BlockSpec((tm, tn), lambda j, i: (i, j)),
        scratch_shapes=[pltpu.VMEM((k, tn), BF16)],
        compiler_params=_params(("arbitrary", "arbitrary")),
        name="matmul_gelu" if gelu else "matmul",
    )(x, w, b)


def _conv_kernel(b_ref, c_ref, v_ref, cp_ref, vp_ref, cn_ref, vn_ref, w_ref, o_ref, *, n_tiles):
    i = pl.program_id(0)
    tm = b_ref.shape[0]
    u = c_ref[...].astype(F32) * v_ref[...].astype(F32)
    u_prev = cp_ref[7:8, :].astype(F32) * vp_ref[7:8, :].astype(F32)
    u_next = cn_ref[0:1, :].astype(F32) * vn_ref[0:1, :].astype(F32)
    u_prev = jnp.where(i == 0, 0.0, u_prev)
    u_next = jnp.where(i == n_tiles - 1, 0.0, u_next)
    row = lax.broadcasted_iota(I32, u.shape, 0)
    below = jnp.where(row == 0, u_prev, pltpu.roll(u, 1, axis=0))
    above = jnp.where(row == tm - 1, u_next, pltpu.roll(u, tm - 1, axis=0))
    w = w_ref[...]
    y = w[0:1, :] * below + w[1:2, :] * u + w[2:3, :] * above
    o_ref[...] = (b_ref[...].astype(F32) * y).astype(BF16)


def _short_conv_gate(bcv, conv_w):
    m, n3 = bcv.shape
    d = n3 // 3
    tm = 512 if m % 512 == 0 else 256
    tc = 1024 if d % 1024 == 0 else 512
    nt, nc = m // tm, d // tc
    hb = tm // 8
    last8 = m // 8 - 1
    return pl.pallas_call(
        functools.partial(_conv_kernel, n_tiles=nt),
        out_shape=jax.ShapeDtypeStruct((m, d), BF16),
        grid=(nt, nc),
        in_specs=[
            pl.BlockSpec((tm, tc), lambda i, j: (i, j)),
            pl.BlockSpec((tm, tc), lambda i, j: (i, j + nc)),
            pl.BlockSpec((tm, tc), lambda i, j: (i, j + 2 * nc)),
            pl.BlockSpec((8, tc), lambda i, j: (jnp.maximum(i * hb - 1, 0), j + nc)),
            pl.BlockSpec((8, tc), lambda i, j: (jnp.maximum(i * hb - 1, 0), j + 2 * nc)),
            pl.BlockSpec((8, tc), lambda i, j: (jnp.minimum((i + 1) * hb, last8), j + nc)),
            pl.BlockSpec((8, tc), lambda i, j: (jnp.minimum((i + 1) * hb, last8), j + 2 * nc)),
            pl.BlockSpec((CONV_W, tc), lambda i, j: (0, j)),
        ],
        out_specs=pl.BlockSpec((tm, tc), lambda i, j: (i, j)),
        compiler_params=_params(("arbitrary", "arbitrary")),
        name="short_conv_gate",
    )(bcv, bcv, bcv, bcv, bcv, bcv, bcv, conv_w)


def _head_sumsq(x, seg_ref):
    x2 = x * x
    hi = x2.astype(BF16)
    lo = (x2 - hi.astype(F32)).astype(BF16)
    seg = seg_ref[...]
    return (jnp.dot(hi, seg, preferred_element_type=F32)
            + jnp.dot(lo, seg, preferred_element_type=F32))


def _swap16(x):
    n = x.shape[-1]
    lane = lax.broadcasted_iota(I32, x.shape, x.ndim - 1)
    first = (lane % 32) < 16
    return jnp.where(first, pltpu.roll(x, n - 16, axis=x.ndim - 1), pltpu.roll(x, 16, axis=x.ndim - 1))


def _tile_lanes(x, width):
    return jnp.concatenate([x] * (width // x.shape[-1]), axis=-1)


def _kprep_kernel(k_ref, v_ref, g_ref, cos_ref, sin_ref, seg_ref, kd_ref, vlr_ref, *, rope):
    k = k_ref[...].astype(F32)
    width = k.shape[-1]
    ss = _head_sumsq(k, seg_ref)
    kn = k * lax.rsqrt(ss * (1.0 / HEAD_DIM) + NORM_EPS) * _tile_lanes(g_ref[...], width)
    if rope:
        kn = kn * _tile_lanes(cos_ref[...], width) + _swap16(kn) * _tile_lanes(sin_ref[...], width)
    v = v_ref[...].astype(F32)
    lane = lax.broadcasted_iota(I32, (k.shape[0], LANES), 1)
    left = lane < HEAD_DIM
    for p in range(width // LANES):
        kp = kn[:, p * LANES:(p + 1) * LANES]
        kr = pltpu.roll(kp, HEAD_DIM, axis=1)
        kd_ref[:, (2 * p) * LANES:(2 * p + 1) * LANES] = jnp.where(left, kp, kr).astype(BF16)
        kd_ref[:, (2 * p + 1) * LANES:(2 * p + 2) * LANES] = jnp.where(left, kr, kp).astype(BF16)
        vp = v[:, p * LANES:(p + 1) * LANES]
        vr = pltpu.roll(vp, HEAD_DIM, axis=1)
        base = 4 * p * LANES
        vlr_ref[:, base:base + LANES] = jnp.where(left, vp, 0.0).astype(BF16)
        vlr_ref[:, base + LANES:base + 2 * LANES] = jnp.where(left, 0.0, vr).astype(BF16)
        vlr_ref[:, base + 2 * LANES:base + 3 * LANES] = jnp.where(left, vr, 0.0).astype(BF16)
        vlr_ref[:, base + 3 * LANES:base + 4 * LANES] = jnp.where(left, 0.0, vp).astype(BF16)


def _key_value_prep(qkv, g_k, cos, sin, seg, *, q_width, kv_width, rope):
    m = qkv.shape[0]
    tm = 256
    n_kv = kv_width // HEAD_DIM
    kb = q_width // kv_width
    return pl.pallas_call(
        functools.partial(_kprep_kernel, rope=rope),
        out_shape=(jax.ShapeDtypeStruct((m, n_kv * LANES), BF16),
                   jax.ShapeDtypeStruct((m, n_kv * 2 * LANES), BF16)),
        grid=(m // tm,),
        in_specs=[
            pl.BlockSpec((tm, kv_width), lambda i: (i, kb)),
            pl.BlockSpec((tm, kv_width), lambda i: (i, kb + 1)),
            pl.BlockSpec((1, LANES), lambda i: (0, 0)),
            pl.BlockSpec((tm, LANES), lambda i: (i, 0)),
            pl.BlockSpec((tm, LANES), lambda i: (i, 0)),
            pl.BlockSpec((kv_width, kv_width), lambda i: (0, 0)),
        ],
        out_specs=(pl.BlockSpec((tm, n_kv * LANES), lambda i: (i, 0)),
                   pl.BlockSpec((tm, n_kv * 2 * LANES), lambda i: (i, 0))),
        compiler_params=_params(("arbitrary",)),
        name="key_value_prep_rope" if rope else "key_value_prep",
    )(qkv, qkv, g_k, cos, sin, seg)


def _attn_kernel(sink_ref, q_ref, cos_ref, sin_ref, g_ref, seg_ref,
                 kd0_ref, kd1_ref, kd2_ref, vl0_ref, vl1_ref, vl2_ref, kdc_ref, vlc_ref,
                 o_ref, s_scr, p_scr, *, seq, n_kv):
    n = pl.program_id(0)
    blk = ATTN_BLOCK
    n_band = 3 * blk
    n_keys = s_scr.shape[1]
    gw = GQA_GROUP * HEAD_DIM
    r = lax.broadcasted_iota(I32, (blk, 3 * blk), 0)
    kk = lax.broadcasted_iota(I32, (blk, 3 * blk), 1)
    off = kk - r
    key_pos = n * blk + kk - blk
    mask = ((off >= blk - WINDOW) & (off <= blk + WINDOW) & (key_pos >= 0) & (key_pos < seq))
    lane = lax.broadcasted_iota(I32, (blk, LANES), 1)
    left = lane < HEAD_DIM
    cos = _tile_lanes(cos_ref[...], gw)
    sin = _tile_lanes(sin_ref[...], gw)
    gq = _tile_lanes(g_ref[...], gw)
    nt = (((1,), (1,)), ((), ()))
    for h in range(n_kv):
        q = q_ref[:, h * gw:(h + 1) * gw].astype(F32)
        ss = _head_sumsq(q, seg_ref)
        qn = q * lax.rsqrt(ss * (1.0 / HEAD_DIM) + NORM_EPS) * gq
        qn = ((qn * cos + _swap16(qn) * sin) * (ATTN_SCALE * LOG2_E)).astype(BF16)
        kd = jnp.concatenate([ref[:, h * LANES:(h + 1) * LANES] for ref in (kd0_ref, kd1_ref, kd2_ref)], axis=0)
        kdc = kdc_ref[:, h * LANES:(h + 1) * LANES]
        vb = 2 * h * LANES
        vl = jnp.concatenate([ref[:, vb:vb + LANES] for ref in (vl0_ref, vl1_ref, vl2_ref)], axis=0)
        vr = jnp.concatenate([ref[:, vb + LANES:vb + 2 * LANES] for ref in (vl0_ref, vl1_ref, vl2_ref)], axis=0)
        vlc = vlc_ref[:, vb:vb + LANES]
        vrc = vlc_ref[:, vb + LANES:vb + 2 * LANES]

        for g in range(GQA_GROUP):
            pair = qn[:, (g // 2) * LANES:(g // 2 + 1) * LANES]
            qg = jnp.where(left, pair, 0.0) if g % 2 == 0 else jnp.where(left, 0.0, pair)
            s = lax.dot_general(qg, kd, nt, preferred_element_type=F32)
            s_scr[g * blk:(g + 1) * blk, 0:n_band] = jnp.where(mask, s, NEG_INF)
            s_scr[g * blk:(g + 1) * blk, n_band:n_keys] = lax.dot_general(qg, kdc, nt, preferred_element_type=F32)

        sk = jnp.concatenate([jnp.full((blk, 1), sink_ref[h * GQA_GROUP + g] * LOG2_E, F32)
                              for g in range(GQA_GROUP)], axis=0)
        mx = jnp.maximum(jnp.max(s_scr[...], axis=-1, keepdims=True), sk)
        e = jnp.exp2(s_scr[...] - mx)
        s_scr[...] = e
        inv = 1.0 / (jnp.sum(e, axis=-1, keepdims=True) + jnp.exp2(sk - mx))
        p_scr[...] = (s_scr[...] * inv).astype(BF16)

        for p in range(GQA_GROUP // 2):
            r0, r1, r2 = 2 * p * blk, (2 * p + 1) * blk, (2 * p + 2) * blk
            acc = (jnp.dot(p_scr[r0:r1, 0:n_band], vl, preferred_element_type=F32)
                   + jnp.dot(p_scr[r0:r1, n_band:n_keys], vlc, preferred_element_type=F32)
                   + jnp.dot(p_scr[r1:r2, 0:n_band], vr, preferred_element_type=F32)
                   + jnp.dot(p_scr[r1:r2, n_band:n_keys], vrc, preferred_element_type=F32))
            o_ref[:, h * gw + p * LANES:h * gw + (p + 1) * LANES] = acc.astype(BF16)


def _window_attention(qkv, kd, vlr, kdc, vlrc, sink, g_q, cos, sin, seg, *, q_width):
    seq = qkv.shape[0]
    n_kv = kd.shape[1] // LANES
    nb = seq // ATTN_BLOCK
    n_ctx = kdc.shape[0]
    gw = GQA_GROUP * HEAD_DIM

    def band(shift, width):
        return pl.BlockSpec((ATTN_BLOCK, width),
                            lambda n, s: (jnp.clip(n + shift, 0, nb - 1), 0))

    grid_spec = pltpu.PrefetchScalarGridSpec(
        num_scalar_prefetch=1,
        grid=(nb,),
        in_specs=[
            pl.BlockSpec((ATTN_BLOCK, q_width), lambda n, s: (n, 0)),
            pl.BlockSpec((ATTN_BLOCK, LANES), lambda n, s: (n, 0)),
            pl.BlockSpec((ATTN_BLOCK, LANES), lambda n, s: (n, 0)),
            pl.BlockSpec((1, LANES), lambda n, s: (0, 0)),
            pl.BlockSpec((gw, gw), lambda n, s: (0, 0)),
            band(-1, kd.shape[1]), band(0, kd.shape[1]), band(1, kd.shape[1]),
            band(-1, vlr.shape[1]), band(0, vlr.shape[1]), band(1, vlr.shape[1]),
            pl.BlockSpec((n_ctx, kdc.shape[1]), lambda n, s: (0, 0)),
            pl.BlockSpec((n_ctx, vlrc.shape[1]), lambda n, s: (0, 0)),
        ],
        out_specs=pl.BlockSpec((ATTN_BLOCK, q_width), lambda n, s: (n, 0)),
        scratch_shapes=[pltpu.VMEM((GQA_GROUP * ATTN_BLOCK, 3 * ATTN_BLOCK + n_ctx), F32),
                        pltpu.VMEM((GQA_GROUP * ATTN_BLOCK, 3 * ATTN_BLOCK + n_ctx), BF16)],
    )
    return pl.pallas_call(
        functools.partial(_attn_kernel, seq=seq, n_kv=n_kv),
        out_shape=jax.ShapeDtypeStruct((seq, q_width), BF16),
        grid_spec=grid_spec,
        compiler_params=_params(("arbitrary",)),
        name="window_attention",
    )(sink, qkv, cos, sin, g_q, seg, kd, kd, kd, vlr, vlr, vlr, kdc, vlrc)


def _sgu_kernel(u_ref, v_ref, g_ref, ws_ref, bs_ref, o_ref):
    v = v_ref[...].astype(F32)
    vn = (v * _rms_scale(v) * g_ref[...]).astype(BF16)
    d = v.shape[-1]
    gwid = d // N_SPATIAL_GROUPS
    for g in range(N_SPATIAL_GROUPS):
        cols = slice(g * gwid, (g + 1) * gwid)
        vs = jnp.dot(ws_ref[g].astype(BF16), vn[:, cols], preferred_element_type=F32) + bs_ref[:, g:g + 1]
        o_ref[:, cols] = (u_ref[:, cols].astype(F32) * vs).astype(BF16)


def _spatial_gate(uv, g_v, w_s, b_s_t):
    m, d2 = uv.shape
    d = d2 // 2
    return pl.pallas_call(
        _sgu_kernel,
        out_shape=jax.ShapeDtypeStruct((m, d), BF16),
        grid=(m // CHUNK,),
        in_specs=[
            pl.BlockSpec((CHUNK, d), lambda i: (i, 0)),
            pl.BlockSpec((CHUNK, d), lambda i: (i, 1)),
            pl.BlockSpec((1, d), lambda i: (0, 0)),
            pl.BlockSpec((N_SPATIAL_GROUPS, CHUNK, CHUNK), lambda i: (0, 0, 0)),
            pl.BlockSpec((CHUNK, N_SPATIAL_GROUPS), lambda i: (0, 0)),
        ],
        out_specs=pl.BlockSpec((CHUNK, d), lambda i: (i, 0)),
        compiler_params=_params(("arbitrary",)),
        name="spatial_gate",
    )(uv, uv, g_v, w_s, b_s_t)


def _out_router_kernel(z_ref, w_ref, b_ref, h_ref, gate_ref, g_ref, sh_ref, sc_ref,
                       rw_ref, rb_ref, cnt0_ref,
                       h1_ref, xw_ref, idx_ref, gts_ref, rank_ref, cnt_ref, run_ref):
    i = pl.program_id(0)
    tm, d = h_ref.shape
    n_exp = rw_ref.shape[0]

    @pl.when(i == 0)
    def _():
        run_ref[...] = cnt0_ref[...]

    out = jnp.dot(z_ref[...], w_ref[...], preferred_element_type=F32) + b_ref[...]
    h1 = h_ref[...] + gate_ref[...] * out
    h1_ref[...] = h1
    y = _modulate(h1, g_ref[...], sh_ref[...], sc_ref[...])
    yb = y.astype(BF16)
    _store_row_tiles(xw_ref, y)

    logits = lax.dot_general(rw_ref[...], yb, (((1,), (1,)), ((), ())),
                             preferred_element_type=F32) + rb_ref[:, 0:1]
    e_iota = lax.broadcasted_iota(I32, (n_exp, tm), 0).astype(F32)
    vals, idxs = [], []
    cur = logits
    for _k in range(TOP_K):
        mx = jnp.max(cur, axis=0, keepdims=True)
        ix = jnp.min(jnp.where(cur == mx, e_iota, float(n_exp)), axis=0, keepdims=True)
        vals.append(mx)
        idxs.append(ix)
        cur = jnp.where(e_iota == ix, -jnp.inf, cur)
    exps = [jnp.exp(v - vals[0]) for v in vals]
    den = exps[0] + exps[1] + exps[2] + exps[3]
    sel = jnp.zeros((n_exp, tm), F32)
    for ix in idxs:
        sel = sel + jnp.where(e_iota == ix, 1.0, 0.0)
    before = (lax.broadcasted_iota(I32, (tm, tm), 0) < lax.broadcasted_iota(I32, (tm, tm), 1))
    prior = jnp.dot(sel.astype(BF16), jnp.where(before, 1.0, 0.0).astype(BF16),
                    preferred_element_type=F32) + run_ref[:, 0:1]
    zeros = jnp.zeros((8 - TOP_K, tm), F32)
    ranks = [jnp.sum(jnp.where(e_iota == ix, prior, 0.0), axis=0, keepdims=True) for ix in idxs]
    idx_ref[...] = jnp.concatenate(idxs + [zeros], axis=0).astype(I32)
    gts_ref[...] = jnp.concatenate([e / den for e in exps] + [zeros], axis=0)
    rank_ref[...] = jnp.concatenate(ranks + [zeros], axis=0).astype(I32)
    run_ref[...] = run_ref[...] + jnp.sum(sel, axis=1, keepdims=True)
    cnt_ref[...] = run_ref[...]


def _out_proj_router(z, w, b, h, gate, g2, sh2, sc2, r_w_t, r_b, cnt0):
    m, d = h.shape
    k = z.shape[1]
    n_exp = r_w_t.shape[0]
    tm = 256
    nch = d // LANES
    vec = pl.BlockSpec((1, d), lambda i: (0, 0))
    row = lambda width: pl.BlockSpec((tm, width), lambda i: (i, 0))
    tok = pl.BlockSpec((8, tm), lambda i: (0, i))
    cnt = pl.BlockSpec((n_exp, LANES), lambda i: (0, 0))
    return pl.pallas_call(
        _out_router_kernel,
        out_shape=(jax.ShapeDtypeStruct((m, d), F32),
                   jax.ShapeDtypeStruct((m * nch, LANES), F32),
                   jax.ShapeDtypeStruct((8, m), I32),
                   jax.ShapeDtypeStruct((8, m), F32),
                   jax.ShapeDtypeStruct((8, m), I32),
                   jax.ShapeDtypeStruct((n_exp, LANES), F32)),
        grid=(m // tm,),
        in_specs=[row(k), pl.BlockSpec((k, d), lambda i: (0, 0)), vec, row(d), vec, vec, vec, vec,
                  pl.BlockSpec((n_exp, d), lambda i: (0, 0)), cnt, cnt],
        out_specs=(row(d), pl.BlockSpec((tm * nch, LANES), lambda i: (i, 0)), tok, tok, tok, cnt),
        scratch_shapes=[pltpu.VMEM((n_exp, LANES), F32)],
        compiler_params=_params(("arbitrary",)),
        name="out_proj_router",
    )(z, w, b, h, gate, g2, sh2, sc2, r_w_t, r_b, cnt0)


GATHER_UNROLL = 16
WEIGHT_CHUNKS = 4


def _gather_tokens(idx_ref, src_hbm, dst_ref, sem, n_tok, nch, *, unroll, priorities=(0, 1)):
    pitch = _gather_pitch(nch)

    def issue(r, u):
        t = idx_ref[0, 0, r]
        src = src_hbm.at[pl.ds(pl.multiple_of(t * nch, 8), nch)]
        dst = dst_ref.at[pl.ds(pl.multiple_of(r * pitch, 8), nch)]
        pltpu.make_async_copy(src, dst, sem).start(priority=priorities[u % len(priorities)])

    if unroll >= n_tok:
        for r in range(n_tok):
            issue(r, r)
        return

    def body(c, carry):
        for u in range(unroll):
            issue(c * unroll + u, u)
        return carry
    lax.fori_loop(0, n_tok // unroll, body, 0)


def _wait_tokens(src_hbm, dst_ref, sem, n_tok, nch):
    n_rows = n_tok * nch
    pltpu.make_async_copy(src_hbm.at[pl.ds(0, n_rows)], dst_ref.at[pl.ds(0, n_rows)], sem).wait()


def _moe_ffn_kernel(be_ref, nu_ref, nx_ref, first_ref, next_ref, x_hbm, w1_hbm, w2_hbm, b1_ref, b2_ref,
                    o_ref, xbuf, gsem, w1s, w2s, wsem, w1b, w2p, pair, *, layer, nch):
    b = pl.program_id(0)
    n_used = nu_ref[0]
    rows = MOE_ROWS
    slot = b % 2

    def chunk_copies(w_hbm, stage, sem, e):
        n = stage.shape[0] // WEIGHT_CHUNKS
        return [pltpu.make_async_copy(w_hbm.at[layer, e, pl.ds(i * n, n)], stage.at[pl.ds(i * n, n)], sem)
                for i in range(WEIGHT_CHUNKS)]

    def start_all(copies):
        for i, cp in enumerate(copies):
            cp.start(priority=i % 2)

    def wait_all(copies):
        for cp in copies:
            cp.wait()

    @pl.when(b == 0)
    def _():
        start_all(chunk_copies(w1_hbm, w1s, wsem.at[0], be_ref[0]))
        start_all(chunk_copies(w2_hbm, w2s, wsem.at[1], be_ref[0]))
        _gather_tokens(first_ref, x_hbm, xbuf.at[0], gsem.at[0], rows, nch, unroll=GATHER_UNROLL)

    @pl.when(b >= n_used)
    def _():
        o_ref[...] = jnp.zeros(o_ref.shape, o_ref.dtype)

    @pl.when(b < n_used)
    def _():
        e = be_ref[b]
        changed = jnp.logical_or(b == 0, e != be_ref[jnp.maximum(b - 1, 0)])

        @pl.when(changed)
        def _():
            nxt = nx_ref[b]
            wait_all(chunk_copies(w1_hbm, w1s, wsem.at[0], e))
            w1b[...] = w1s[...].astype(BF16)

            @pl.when(nxt >= 0)
            def _():
                start_all(chunk_copies(w1_hbm, w1s, wsem.at[0], nxt))

            wait_all(chunk_copies(w2_hbm, w2s, wsem.at[1], e))
            half = LANES // 2
            for m in range(w2p.shape[0] // LANES):
                for c in range(nch):
                    cols = slice(c * LANES, (c + 1) * LANES)
                    pair[pl.ds(c * LANES, half, stride=2), :] = w2s[m * LANES:m * LANES + half, cols]
                    pair[pl.ds(c * LANES + 1, half, stride=2), :] = w2s[m * LANES + half:(m + 1) * LANES, cols]
                    w2p[m * LANES:(m + 1) * LANES, cols] = pair[c * LANES:(c + 1) * LANES, :].astype(BF16)

            @pl.when(nxt >= 0)
            def _():
                start_all(chunk_copies(w2_hbm, w2s, wsem.at[1], nxt))

        _gather_tokens(next_ref, x_hbm, xbuf.at[1 - slot], gsem.at[1 - slot], rows, nch, unroll=rows)
        _wait_tokens(x_hbm, xbuf.at[slot], gsem.at[slot], rows, nch)
        x = _load_row_tiles(xbuf.at[slot], 0, rows, nch, BF16, _gather_pitch(nch))
        hcat = jnp.dot(x, w1b[...], preferred_element_type=F32) + b1_ref[0, 0]
        glu = jnp.minimum(hcat, SWIGLU_LIMIT)
        gact = glu * _sigmoid(SWIGLU_ALPHA * glu)
        lin = jnp.clip(hcat, -SWIGLU_LIMIT, SWIGLU_LIMIT) + 1.0
        lane = lax.broadcasted_iota(I32, (rows, LANES), 1)
        even = (lane % 2) == 0
        acts = []
        for m in range(hcat.shape[1] // (2 * LANES)):
            a0, a1, a2 = 2 * m * LANES, (2 * m + 1) * LANES, (2 * m + 2) * LANES
            pa = gact[:, a0:a1] * pltpu.roll(lin[:, a0:a1], LANES - 1, axis=1)
            pb = pltpu.roll(gact[:, a1:a2], 1, axis=1) * lin[:, a1:a2]
            acts.append(jnp.where(even, pa, pb).astype(BF16))
        act = jnp.concatenate(acts, axis=1)
        y = jnp.dot(act, w2p[...], preferred_element_type=F32) + b2_ref[0, 0]
        _store_row_tiles(o_ref, y)

        @pl.when(b == n_used - 1)
        def _():
            _wait_tokens(x_hbm, xbuf.at[1 - slot], gsem.at[1 - slot], rows, nch)


def _moe_ffn(block_e, n_used, next_e, src_tok, xrows, w1, b1, w2, b2, layer):
    nb = src_tok.shape[0]
    rows = src_tok.shape[2]
    depth, n_exp, d, f2 = w1.shape
    f = f2 // 2
    nch = d // LANES
    last_used = lambda b, be, nu, nx: jnp.minimum(b + 1, nu[0] - 1)
    grid_spec = pltpu.PrefetchScalarGridSpec(
        num_scalar_prefetch=3,
        grid=(nb,),
        in_specs=[
            pl.BlockSpec((1, 1, rows), lambda b, be, nu, nx: (0, 0, 0), memory_space=pltpu.SMEM),
            pl.BlockSpec((1, 1, rows), lambda b, be, nu, nx: (last_used(b, be, nu, nx), 0, 0),
                         memory_space=pltpu.SMEM),
            pl.BlockSpec(memory_space=pl.ANY),
            pl.BlockSpec(memory_space=pl.ANY),
            pl.BlockSpec(memory_space=pl.ANY),
            pl.BlockSpec((1, 1, 1, f2), lambda b, be, nu, nx: (layer, be[b], 0, 0)),
            pl.BlockSpec((1, 1, 1, d), lambda b, be, nu, nx: (layer, be[b], 0, 0)),
        ],
        out_specs=pl.BlockSpec((rows * nch, LANES), lambda b, be, nu, nx: (b, 0)),
        scratch_shapes=[pltpu.VMEM((2, rows * _gather_pitch(nch), LANES), F32),
                        pltpu.SemaphoreType.DMA((2,)),
                        pltpu.VMEM((d, f2), F32),
                        pltpu.VMEM((f, d), F32),
                        pltpu.SemaphoreType.DMA((2,)),
                        pltpu.VMEM((d, f2), BF16),
                        pltpu.VMEM((f, d), BF16),
                        pltpu.VMEM((nch * LANES, LANES), F32)],
    )
    return pl.pallas_call(
        functools.partial(_moe_ffn_kernel, layer=layer, nch=nch),
        out_shape=jax.ShapeDtypeStruct((nb * rows * nch, LANES), F32),
        grid_spec=grid_spec,
        compiler_params=_params(("arbitrary",)),
        name="moe_ffn",
    )(block_e, n_used, next_e, src_tok, src_tok, xrows, w1, w2,
      b1.reshape(depth, n_exp, 1, f2), b2.reshape(depth, n_exp, 1, d))


def _combine_kernel(first_ref, next_ref, y_hbm, gts_ref, h_ref, gate_ref, g_ref, sh_ref, sc_ref,
                    h2_ref, *rest, n_tiles, with_norm):
    if with_norm:
        hn_ref, ybuf, sem = rest
    else:
        ybuf, sem = rest
    i = pl.program_id(0)
    tm, d = h_ref.shape
    nch = d // LANES
    rows = TOP_K * tm
    slot = i % 2

    @pl.when(i == 0)
    def _():
        _gather_tokens(first_ref, y_hbm, ybuf.at[0], sem.at[0], rows, nch, unroll=GATHER_UNROLL)

    @pl.when(i + 1 < n_tiles)
    def _():
        _gather_tokens(next_ref, y_hbm, ybuf.at[1 - slot], sem.at[1 - slot], rows, nch, unroll=GATHER_UNROLL)

    _wait_tokens(y_hbm, ybuf.at[slot], sem.at[slot], rows, nch)
    pitch = _gather_pitch(nch)
    gks = [gts_ref[:, k:k + 1] for k in range(TOP_K)]
    pieces = []
    for j in range(nch):
        acc = None
        for k in range(TOP_K):
            part = ybuf[slot, pl.ds(k * tm * pitch + j, tm, stride=pitch), :] * gks[k]
            acc = part if acc is None else acc + part
        pieces.append(acc)
    h2 = h_ref[...] + gate_ref[...] * jnp.concatenate(pieces, axis=1)
    h2_ref[...] = h2
    if with_norm:
        hn_ref[...] = _modulate(h2, g_ref[...], sh_ref[...], sc_ref[...]).astype(BF16)


def _moe_combine(dest, y, gts_t, h, gate, g1, sh1, sc1, *, with_norm):
    m, d = h.shape
    n_tiles = dest.shape[0]
    tm = m // n_tiles
    vec = pl.BlockSpec((1, d), lambda i: (0, 0))
    row = pl.BlockSpec((tm, d), lambda i: (i, 0))
    out_shape = [jax.ShapeDtypeStruct((m, d), F32)]
    out_specs = [row]
    if with_norm:
        out_shape.append(jax.ShapeDtypeStruct((m, d), BF16))
        out_specs.append(row)
    res = pl.pallas_call(
        functools.partial(_combine_kernel, n_tiles=n_tiles, with_norm=with_norm),
        out_shape=tuple(out_shape),
        grid=(n_tiles,),
        in_specs=[
            pl.BlockSpec((1, 1, TOP_K * tm), lambda i: (0, 0, 0), memory_space=pltpu.SMEM),
            pl.BlockSpec((1, 1, TOP_K * tm), lambda i: (jnp.minimum(i + 1, n_tiles - 1), 0, 0),
                         memory_space=pltpu.SMEM),
            pl.BlockSpec(memory_space=pl.ANY),
            pl.BlockSpec((tm, 8), lambda i: (i, 0)),
            row, vec, vec, vec, vec,
        ],
        out_specs=tuple(out_specs),
        scratch_shapes=[pltpu.VMEM((2, TOP_K * tm * _gather_pitch(d // LANES), LANES), F32),
                        pltpu.SemaphoreType.DMA((2,))],
        compiler_params=_params(("arbitrary",)),
        name="moe_combine_norm" if with_norm else "moe_combine",
    )(dest, dest, y, gts_t, h, gate, g1, sh1, sc1)
    return res if with_norm else (res[0], None)


def _dispatch_plan(idx, rank, counts, rows):
    n_exp = counts.shape[0]
    n_tok = idx.shape[1]
    n_assign = TOP_K * n_tok
    nb = n_assign // rows + n_exp
    blocks_e = (counts + rows - 1) // rows
    blk_end = jnp.cumsum(blocks_e)
    blk_start = blk_end - blocks_e
    n_used = blk_end[-1]
    e_ids = jnp.arange(n_exp, dtype=I32)
    chosen = idx[None, :, :] == e_ids[:, None, None]
    dest = jnp.sum(jnp.where(chosen, blk_start[:, None, None], 0), axis=0) * rows + rank
    blk = jnp.minimum(jnp.arange(nb, dtype=I32), n_used - 1)
    block_e = jnp.minimum(jnp.sum((blk[:, None] >= blk_end[None, :]).astype(I32), axis=1), n_exp - 1)
    group_end = jnp.sum(jnp.where(block_e[:, None] == e_ids[None, :], blk_end[None, :], 0), axis=1)
    follower = jnp.sum(jnp.where(group_end[:, None] == jnp.arange(nb, dtype=I32)[None, :],
                                 block_e[None, :], 0), axis=1)
    next_e = jnp.where(group_end < n_used, follower, -1).astype(I32)
    tok = jnp.broadcast_to(jnp.arange(n_tok, dtype=I32)[None, :], dest.shape)
    src_tok = jnp.zeros((nb * rows,), I32).at[dest.reshape(-1)].set(
        tok.reshape(-1), unique_indices=True, mode='promise_in_bounds')
    return block_e.astype(I32), n_used.reshape(1).astype(I32), next_e, src_tok.reshape(nb, 1, rows), dest


def _tile_dest(dest, tm):
    n_tok = dest.shape[1]
    return dest.reshape(TOP_K, n_tok // tm, tm).transpose(1, 0, 2).reshape(n_tok // tm, 1, TOP_K * tm)


def _rope_tables(seq):
    n_rows = seq // GRID_W
    inv_freq = ROPE_BASE ** (-jnp.arange(ROPE_PAIRS, dtype=F32) / ROPE_PAIRS)
    ang_r = jnp.arange(n_rows, dtype=F32)[:, None] * inv_freq[None, :]
    ang_c = jnp.arange(GRID_W, dtype=F32)[:, None] * inv_freq[None, :]

    def per_token(tab_r, tab_c):
        tr = jnp.broadcast_to(tab_r[:, None, :], (n_rows, GRID_W, tab_r.shape[1])).reshape(seq, -1)
        tc = jnp.broadcast_to(tab_c[None, :, :], (n_rows, GRID_W, tab_c.shape[1])).reshape(seq, -1)
        return jnp.concatenate([tr, tc, tr, tc], axis=1)

    cos_r, sin_r, cos_c, sin_c = jnp.cos(ang_r), jnp.sin(ang_r), jnp.cos(ang_c), jnp.sin(ang_c)
    cos = per_token(jnp.concatenate([cos_r, cos_r], axis=1), jnp.concatenate([cos_c, cos_c], axis=1))
    sin = per_token(jnp.concatenate([-sin_r, sin_r], axis=1), jnp.concatenate([-sin_c, sin_c], axis=1))
    return cos, sin


def _head_segments(width):
    seg = jnp.arange(width, dtype=I32) // HEAD_DIM
    return (seg[:, None] == seg[None, :]).astype(BF16)


def _layer_plan(depth):
    counts = [0] * N_MIXERS
    plan = []
    for i in range(depth):
        kind = i % N_MIXERS
        plan.append((kind, counts[kind]))
        counts[kind] += 1
    return plan


def kernel(x, c, ctx, c_ctx, w_mod, b_mod, g_norm1, g_norm2, a_w_in, a_conv, a_w_out, b_w_qkv, b_b_qkv, b_g_q, b_g_k, b_sink, b_w_o, b_b_o, c_w_in, c_b_in, c_g_v, c_w_s, c_b_s, c_w_out, c_b_out, r_w, r_b, e_w1, e_b1, e_w2, e_b2):
    bsz, seq, d = x.shape
    assert bsz == 1
    depth = w_mod.shape[0]
    n_exp = r_w.shape[2]
    n_ctx = ctx.shape[1]
    plan = _layer_plan(depth)
    readers = [i for i, (kind, _) in enumerate(plan) if kind == 1]
    last_reader = readers[-1] if readers else -1
    q_width = d
    kv_width = (b_w_qkv.shape[2] - q_width) // 2
    rows = MOE_ROWS
    tm_c = 128

    cc = jnp.concatenate([c, c_ctx[None, :], jnp.zeros((6, d), F32)], axis=0)
    mod = _mod_vectors(cc, w_mod, b_mod)

    def mod_vec(layer, stream, chunk):
        return mod[layer, stream:stream + 1, chunk * d:(chunk + 1) * d]

    def vec(a):
        return a.reshape(1, -1)

    cos, sin = _rope_tables(seq)
    seg_q = _head_segments(GQA_GROUP * HEAD_DIM)
    seg_k = _head_segments(kv_width)
    zero_bias = jnp.zeros((1, d), F32)
    zero_cnt = jnp.zeros((n_exp, LANES), F32)

    h = x[0]
    hc = ctx[0]
    hn = _prenorm(h, vec(g_norm1[0]), mod_vec(0, 0, 0), mod_vec(0, 0, 1))
    hn_c = None
    if 0 <= last_reader:
        hn_c = _prenorm(hc, vec(g_norm1[0]), mod_vec(0, 1, 0), mod_vec(0, 1, 1))

    for i, (kind, s) in enumerate(plan):
        upd_ctx = i < last_reader
        read_ctx = i <= last_reader
        streams = [(0, hn, h)] + ([(1, hn_c, hc)] if upd_ctx else [])

        zs = []
        if kind == 0:
            w_out, b_out = a_w_out[s].astype(BF16), zero_bias
            for _, hn_s, _h in streams:
                bcv = _matmul(hn_s, a_w_in, jnp.zeros((1, 3 * d), F32), layer=s)
                zs.append(_short_conv_gate(bcv, a_conv[s]))
        elif kind == 1:
            w_out, b_out = b_w_o[s].astype(BF16), vec(b_b_o[s])
            g_q = vec(jnp.concatenate([b_g_q[s], b_g_q[s]]))
            g_k = vec(jnp.concatenate([b_g_k[s], b_g_k[s]]))
            qkv = _matmul(hn, b_w_qkv, vec(b_b_qkv[s]), layer=s)
            qkv_c = _matmul(hn_c, b_w_qkv, vec(b_b_qkv[s]), layer=s)
            kd, vlr = _key_value_prep(qkv, g_k, cos, sin, seg_k, q_width=q_width, kv_width=kv_width, rope=True)
            kdc, vlrc = _key_value_prep(qkv_c, g_k, cos[:n_ctx], sin[:n_ctx], seg_k,
                                        q_width=q_width, kv_width=kv_width, rope=False)
            zs.append(_window_attention(qkv, kd, vlr, kdc, vlrc, b_sink[s], g_q, cos, sin, seg_q,
                                        q_width=q_width))
            assert not upd_ctx
        else:
            w_out, b_out = c_w_out[s].astype(BF16), vec(c_b_out[s])
            for _, hn_s, _h in streams:
                uv = _matmul(hn_s, c_w_in, vec(c_b_in[s]), layer=s, gelu=True)
                zs.append(_spatial_gate(uv, vec(c_g_v[s]), c_w_s[s], c_b_s[s].T))

        r_w_t = r_w[i].T.astype(BF16)
        r_b_col = jnp.broadcast_to(r_b[i][:, None], (n_exp, LANES))
        outs = []
        cnt = zero_cnt
        for (st, _hn, h_s), z in zip(streams, zs):
            res = _out_proj_router(z, w_out, b_out, h_s, mod_vec(i, st, 2), vec(g_norm2[i]),
                                   mod_vec(i, st, 3), mod_vec(i, st, 4), r_w_t, r_b_col, cnt)
            outs.append(res)
            cnt = res[5]
        if len(outs) == 1:
            h1s, xw, idx, gts, rank = [outs[0][0]], outs[0][1], outs[0][2], outs[0][3], outs[0][4]
        else:
            h1s = [o[0] for o in outs]
            xw, idx, gts, rank = [jnp.concatenate([o[j] for o in outs], axis=(0 if j == 1 else 1))
                                  for j in (1, 2, 3, 4)]
        counts = cnt[:, 0].astype(I32)

        block_e, n_used, next_e, src_tok, dest = _dispatch_plan(idx[:TOP_K], rank[:TOP_K], counts, rows)
        y = _moe_ffn(block_e, n_used, next_e, src_tok, xw, e_w1, e_b1, e_w2, e_b2, i)

        last = i == depth - 1
        new = []
        off = 0
        for (st, _hn, _h), h1 in zip(streams, h1s):
            n_tok = h1.shape[0]
            sl = slice(off, off + n_tok)
            off += n_tok
            nxt = i + 1
            need_norm = (not last) and (st == 0 or nxt <= last_reader)
            if need_norm:
                g1, sh1, sc1 = vec(g_norm1[nxt]), mod_vec(nxt, st, 0), mod_vec(nxt, st, 1)
            else:
                g1, sh1, sc1 = zero_bias, zero_bias, zero_bias
            h2, hn2 = _moe_combine(_tile_dest(dest[:, sl], tm_c), y, gts[:, sl].T,
                                   h1, mod_vec(i, st, 5), g1, sh1, sc1, with_norm=need_norm)
            new.append((h2, hn2))
        h, hn = new[0]
        if upd_ctx:
            hc, hn_c = new[1]
    return h[None]
```
